```python
import math
import jax, jax.numpy as jnp
from jax import lax
import numpy as np

D_MODEL = 1024
BATCH = 2
SEQ = 16384
DEPTH = 2

HEAD_DIM = 64
N_HEADS = D_MODEL // HEAD_DIM
N_MEM_HEADS = 4
N_SELF_HEADS = N_HEADS - N_MEM_HEADS
SELF_WIDTH = N_SELF_HEADS * HEAD_DIM
MEM_WIDTH = N_MEM_HEADS * HEAD_DIM
N_MEM = 256
ROT_DIM = HEAD_DIM // 4
ROPE_THETA = 500000.0
MOBA_BLOCK = 256
MOBA_TOPK = 3
MOBA_QCHUNK = 64
SB_QBLOCK = 128
D_FF = ((8 * D_MODEL // 3 + 127) // 128) * 128
N_MIXERS = 2
RMS_EPS = 1e-6
NEG_INF = -1e30
MAX_POS_OFFSET = 4096

kernel_name = "hybrid_moba_stickbreak_macaron"


def rms_norm(x, g):
    xf = x.astype(jnp.float32)
    y = xf * lax.rsqrt(jnp.mean(xf * xf, axis=-1, keepdims=True) + RMS_EPS)
    return y.astype(x.dtype) * g


def swiglu(u, w_gate, w_up, w_down):
    return (jax.nn.silu(u @ w_gate) * (u @ w_up)) @ w_down


def rope_tables(positions):
    inv_freq = ROPE_THETA ** (-jnp.arange(0, ROT_DIM, 2, dtype=jnp.float32) / ROT_DIM)
    ang = positions.astype(jnp.float32)[..., None] * inv_freq
    return jnp.cos(ang)[:, None], jnp.sin(ang)[:, None]


def apply_partial_rope(x, cos, sin):
    half = ROT_DIM // 2
    x1, x2, rest = x[..., :half], x[..., half:ROT_DIM], x[..., ROT_DIM:]
    c, s = cos.astype(x.dtype), sin.astype(x.dtype)
    return jnp.concatenate([x1 * c - x2 * s, x2 * c + x1 * s, rest], axis=-1)


def moba_attention(q, k, v):
    B, H, S, Dh = q.shape
    nb = -(-S // MOBA_BLOCK)
    topk = min(MOBA_TOPK, nb)
    pad = nb * MOBA_BLOCK - S
    k_blocks = jnp.pad(k, ((0, 0), (0, 0), (0, pad), (0, 0))).reshape(B, H, nb, MOBA_BLOCK, Dh)
    v_blocks = jnp.pad(v, ((0, 0), (0, 0), (0, pad), (0, 0))).reshape(B, H, nb, MOBA_BLOCK, Dh)
    k_mean = jnp.mean(k_blocks.astype(jnp.float32), axis=3)
    scale = 1.0 / math.sqrt(Dh)
    bi = jnp.arange(B)[:, None, None, None]
    hi = jnp.arange(H)[None, :, None, None]
    blk_ids = jnp.arange(nb)

    def chunk(c):
        start = c * MOBA_QCHUNK
        own = start // MOBA_BLOCK
        q_c = lax.dynamic_slice_in_dim(q, start, MOBA_QCHUNK, axis=2) * scale
        gate = jnp.einsum('bhqd,bhnd->bhqn', q_c.astype(jnp.float32), k_mean)
        gate = jnp.where(blk_ids < own, gate, NEG_INF)
        _, idx = lax.top_k(gate, topk)
        valid = jnp.arange(topk) < own
        k_sel = k_blocks[bi, hi, idx]
        v_sel = v_blocks[bi, hi, idx]
        s_sel = jnp.einsum('bhqd,bhqknd->bhqkn', q_c, k_sel).astype(jnp.float32)
        s_sel = jnp.where(valid[:, None], s_sel, NEG_INF).reshape(B, H, MOBA_QCHUNK, topk * MOBA_BLOCK)
        k_own = lax.dynamic_index_in_dim(k_blocks, own, axis=2, keepdims=False)
        v_own = lax.dynamic_index_in_dim(v_blocks, own, axis=2, keepdims=False)
        s_own = jnp.einsum('bhqd,bhnd->bhqn', q_c, k_own).astype(jnp.float32)
        pos_q = start + jnp.arange(MOBA_QCHUNK)
        pos_k = own * MOBA_BLOCK + jnp.arange(MOBA_BLOCK)
        s_own = jnp.where(pos_k[None, :] <= pos_q[:, None], s_own, NEG_INF)
        p = jax.nn.softmax(jnp.concatenate([s_sel, s_own], axis=-1), axis=-1).astype(v.dtype)
        p_sel = p[..., :topk * MOBA_BLOCK].reshape(B, H, MOBA_QCHUNK, topk, MOBA_BLOCK)
        p_own = p[..., topk * MOBA_BLOCK:]
        return (jnp.einsum('bhqkn,bhqknd->bhqd', p_sel, v_sel)
                + jnp.einsum('bhqn,bhnd->bhqd', p_own, v_own))

    out = lax.map(chunk, jnp.arange(S // MOBA_QCHUNK))
    return out.transpose(1, 2, 0, 3, 4).reshape(B, H, S, Dh)


def stick_breaking_attention(q, k, v):
    B, H, S, Dh = q.shape
    scale = 1.0 / math.sqrt(Dh)
    outs = []
    for i in range(S // SB_QBLOCK):
        t0, t1 = i * SB_QBLOCK, (i + 1) * SB_QBLOCK
        z = jnp.einsum('bhqd,bhsd->bhqs', q[:, :, t0:t1] * scale, k[:, :, :t1]).astype(jnp.float32)
        causal = jnp.arange(t1)[None, :] < (t0 + jnp.arange(SB_QBLOCK))[:, None]
        log_1m = jnp.where(causal, jax.nn.log_sigmoid(-z), 0.0)
        suffix = lax.cumsum(log_1m, axis=3, reverse=True) - log_1m
        a = jnp.where(causal, jnp.exp(jax.nn.log_sigmoid(z) + suffix), 0.0)
        outs.append(jnp.einsum('bhqs,bhsd->bhqd', a.astype(v.dtype), v[:, :, :t1]))
    return jnp.concatenate(outs, axis=2)


def memory_attention(qm, km, vm):
    s = jnp.einsum('bhqd,bhnd->bhqn', qm * (1.0 / math.sqrt(qm.shape[-1])), km).astype(jnp.float32)
    p = jax.nn.softmax(s, axis=-1).astype(vm.dtype)
    return jnp.einsum('bhqn,bhnd->bhqd', p, vm)


def token_mixing(u, mem, cos, sin, w_in, g_mem, w_mem_kv, w_out, mixer_id):
    B, S, _ = u.shape
    proj = u @ w_in
    q, k, v, qm = jnp.split(proj, [SELF_WIDTH, 2 * SELF_WIDTH, 3 * SELF_WIDTH], axis=-1)

    def heads(t, n):
        return t.reshape(B, t.shape[1], n, HEAD_DIM).transpose(0, 2, 1, 3)

    q, k, v = heads(q, N_SELF_HEADS), heads(k, N_SELF_HEADS), heads(v, N_SELF_HEADS)
    if mixer_id == 0:
        o_self = moba_attention(apply_partial_rope(q, cos, sin), apply_partial_rope(k, cos, sin), v)
    else:
        o_self = stick_breaking_attention(q, k, v)
    kvm = rms_norm(mem, g_mem) @ w_mem_kv
    km, vm = jnp.split(kvm, 2, axis=-1)
    o_mem = memory_attention(heads(qm, N_MEM_HEADS), heads(km, N_MEM_HEADS), heads(vm, N_MEM_HEADS))
    o = jnp.concatenate([o_self, o_mem], axis=1).transpose(0, 2, 1, 3).reshape(B, S, D_MODEL)
    return o @ w_out


def setup_inputs(seed: int = 0) -> dict:
    key = jax.random.key(seed)
    ks = jax.random.split(key, 20)
    f32 = jnp.float32

    def w(k, shape, fan_in):
        return jax.random.normal(k, shape, f32) * fan_in ** -0.5

    def gain(k, shape):
        return 1.0 + 0.01 * jax.random.normal(k, shape, f32)

    proj_width = 3 * SELF_WIDTH + MEM_WIDTH
    positions = (jnp.arange(SEQ, dtype=jnp.int32)[None, :]
                 + jax.random.randint(ks[2], (BATCH, 1), 0, MAX_POS_OFFSET, dtype=jnp.int32))
    return {
        "x": jax.random.normal(ks[0], (BATCH, SEQ, D_MODEL), f32),
        "mem": jax.random.normal(ks[1], (BATCH, N_MEM, D_MODEL), f32),
        "positions": positions,
        "g_ffn_pre": gain(ks[3], (DEPTH, D_MODEL)),
        "w_pre_gate": w(ks[4], (DEPTH, D_MODEL, D_FF), D_MODEL),
        "w_pre_up": w(ks[5], (DEPTH, D_MODEL, D_FF), D_MODEL),
        "w_pre_down": w(ks[6], (DEPTH, D_FF, D_MODEL), D_FF),
        "g_mix": gain(ks[7], (DEPTH, D_MODEL)),
        "w_in": w(ks[8], (DEPTH, D_MODEL, proj_width), D_MODEL),
        "g_mem": gain(ks[9], (DEPTH, D_MODEL)),
        "w_mem_kv": w(ks[10], (DEPTH, D_MODEL, 2 * MEM_WIDTH), D_MODEL),
        "w_out": w(ks[11], (DEPTH, D_MODEL, D_MODEL), D_MODEL),
        "g_ffn_post": gain(ks[12], (DEPTH, D_MODEL)),
        "w_post_gate": w(ks[13], (DEPTH, D_MODEL, D_FF), D_MODEL),
        "w_post_up": w(ks[14], (DEPTH, D_MODEL, D_FF), D_MODEL),
        "w_post_down": w(ks[15], (DEPTH, D_FF, D_MODEL), D_FF),
        "g_final": gain(ks[16], (D_MODEL,)),
    }


def reference(x, mem, positions, g_ffn_pre, w_pre_gate, w_pre_up, w_pre_down, g_mix, w_in,
              g_mem, w_mem_kv, w_out, g_ffn_post, w_post_gate, w_post_up, w_post_down, g_final):
    cos, sin = rope_tables(positions)
    h = x
    for layer in range(DEPTH):
        h = h + 0.5 * swiglu(rms_norm(h, g_ffn_pre[layer]), w_pre_gate[layer], w_pre_up[layer], w_pre_down[layer])
        h = h + token_mixing(rms_norm(h, g_mix[layer]), mem, cos, sin, w_in[layer], g_mem[layer],
                             w_mem_kv[layer], w_out[layer], layer % N_MIXERS)
        h = h + 0.5 * swiglu(rms_norm(h, g_ffn_post[layer]), w_post_gate[layer], w_post_up[layer], w_post_down[layer])
    return rms_norm(h, g_final)
```

```python
import functools
import math

import numpy as np
import jax
import jax.numpy as jnp
from jax import lax
from jax.experimental import pallas as pl
from jax.experimental.pallas import tpu as pltpu

F32 = jnp.float32
BF16 = jnp.bfloat16

HEAD_DIM = 64
N_SELF_HEADS = 12
N_MEM_HEADS = 4
SELF_WIDTH = N_SELF_HEADS * HEAD_DIM
MEM_WIDTH = N_MEM_HEADS * HEAD_DIM
ROT_HALF = 8
ROPE_THETA = 500000.0
MOBA_BLOCK = 256
MOBA_TOPK = 3
RMS_EPS = 1e-6
NEG_BIAS = -1e30
QK_SCALE = 1.0 / math.sqrt(HEAD_DIM)

FFN_ROWS = 512
FFN_CHUNK = 256
ATT_TILE = 256
SB_DEAD_LOG = -110.0
VMEM_LIMIT = 56 * 1024 * 1024

_TWO_PI = 2.0 * math.pi
_C1 = 6.28125
_C2 = round((_TWO_PI - _C1) * 2 ** 20) / 2 ** 20
_C3 = float(np.float32(_TWO_PI - _C1 - _C2))
_INV_TWO_PI = float(np.float32(1.0 / _TWO_PI))


def _rms_scale(x):
    return x * lax.rsqrt(jnp.mean(x * x, axis=-1, keepdims=True) + RMS_EPS)


def _dot_t(a, b, **kw):
    return lax.dot_general(a, b, (((0,), (0,)), ((), ())), preferred_element_type=F32, **kw)


def _dot_nt(a, b):
    return lax.dot_general(a, b, (((1,), (1,)), ((), ())), preferred_element_type=F32)


def _ffn_body(*refs, n_chunks, with_attn, with_final):
    it = iter(refs)
    x_ref = next(it)
    if with_attn:
        os_ref, om_ref, wos_ref, wom_ref = next(it), next(it), next(it), next(it)
    g_ref, wg_ref, wu_ref, wd_ref = next(it), next(it), next(it), next(it)
    gf_ref = next(it) if with_final else None
    o_ref, x_scr, u_scr, acc_scr = next(it), next(it), next(it), next(it)

    x = x_ref[...]
    if with_attn:
        x = x + _dot_t(os_ref[0], wos_ref[...]) + _dot_t(om_ref[0], wom_ref[...])
    x_scr[...] = x
    u_scr[...] = (_rms_scale(x) * g_ref[...]).astype(BF16)
    acc_scr[...] = jnp.zeros_like(acc_scr)

    def chunk(c, carry):
        u = u_scr[...]
        gate = jnp.dot(u, wg_ref[c], preferred_element_type=F32)
        up = jnp.dot(u, wu_ref[c], preferred_element_type=F32)
        act = (gate * jax.nn.sigmoid(gate)) * up
        acc_scr[...] += jnp.dot(act.astype(BF16), wd_ref[c], preferred_element_type=F32)
        return carry

    lax.fori_loop(0, n_chunks, chunk, 0)
    y = x_scr[...] + 0.5 * acc_scr[...]
    if with_final:
        y = _rms_scale(y) * gf_ref[...]
    o_ref[...] = y


def _ffn(h, g, wg, wu, wd, attn=None, final_g=None):
    B, S, D = h.shape
    n_chunks, _, chunk_w = wg.shape
    tm = FFN_ROWS
    assert S % tm == 0
    grid = (B, S // tm)
    const2 = lambda b, i: (0, 0)
    const3 = lambda b, i: (0, 0, 0)
    resident = dict(pipeline_mode=pl.Buffered(1))
    in_specs = [pl.BlockSpec((None, tm, D), lambda b, i: (b, i, 0))]
    args = [h]
    if attn is not None:
        o_self, o_mem, w_os, w_om = attn
        in_specs += [
            pl.BlockSpec((1, SELF_WIDTH, tm), lambda b, i: (b, 0, i)),
            pl.BlockSpec((1, MEM_WIDTH, tm), lambda b, i: (b, 0, i)),
            pl.BlockSpec(w_os.shape, const2, **resident),
            pl.BlockSpec(w_om.shape, const2, **resident),
        ]
        args += [o_self, o_mem, w_os, w_om]
    in_specs += [
        pl.BlockSpec((1, D), const2, **resident),
        pl.BlockSpec(wg.shape, const3, **resident),
        pl.BlockSpec(wu.shape, const3, **resident),
        pl.BlockSpec(wd.shape, const3, **resident),
    ]
    args += [g.reshape(1, D), wg, wu, wd]
    if final_g is not None:
        in_specs.append(pl.BlockSpec((1, D), const2, **resident))
        args.append(final_g.reshape(1, D))
    body = functools.partial(_ffn_body, n_chunks=n_chunks, with_attn=attn is not None,
                             with_final=final_g is not None)
    return pl.pallas_call(
        body,
        out_shape=jax.ShapeDtypeStruct((B, S, D), F32),
        grid=grid,
        in_specs=in_specs,
        out_specs=pl.BlockSpec((None, tm, D), lambda b, i: (b, i, 0)),
        scratch_shapes=[pltpu.VMEM((tm, D), F32), pltpu.VMEM((tm, D), BF16), pltpu.VMEM((tm, D), F32)],
        compiler_params=pltpu.CompilerParams(dimension_semantics=("arbitrary", "arbitrary"),
                                             vmem_limit_bytes=VMEM_LIMIT),
        name="ffn_attn" if attn is not None else "ffn",
    )(*args)


def _memkv_body(mem_ref, g_ref, w_ref, o_ref):
    u = (_rms_scale(mem_ref[0]) * g_ref[...]).astype(BF16)
    o_ref[0] = _dot_nt(w_ref[...], u).astype(BF16)


def _memkv(mem, g_mem, w_kv_t):
    B, n_mem, D = mem.shape
    return pl.pallas_call(
        _memkv_body,
        out_shape=jax.ShapeDtypeStruct((B, 2 * MEM_WIDTH, n_mem), BF16),
        grid=(B,),
        in_specs=[pl.BlockSpec((1, n_mem, D), lambda b: (b, 0, 0)),
                  pl.BlockSpec((1, D), lambda b: (0, 0)),
                  pl.BlockSpec(w_kv_t.shape, lambda b: (0, 0))],
        out_specs=pl.BlockSpec((1, 2 * MEM_WIDTH, n_mem), lambda b: (b, 0, 0)),
        name="memkv",
    )(mem, g_mem.reshape(1, -1), w_kv_t)


def _memory_attention(p_scr, kvm_ref, om_ref):
    for hm in range(N_MEM_HEADS):
        r0 = 3 * SELF_WIDTH + hm * HEAD_DIM
        qm = (p_scr[r0:r0 + HEAD_DIM, :] * QK_SCALE).astype(BF16)
        km = kvm_ref[0, hm * HEAD_DIM:(hm + 1) * HEAD_DIM, :]
        vm = kvm_ref[0, MEM_WIDTH + hm * HEAD_DIM:MEM_WIDTH + (hm + 1) * HEAD_DIM, :]
        s = _dot_t(km, qm)
        p = jnp.exp(s - jnp.max(s, axis=0, keepdims=True))
        l = jnp.sum(p, axis=0, keepdims=True)
        o = jnp.dot(vm, p.astype(BF16), preferred_element_type=F32)
        om_ref[0, hm * HEAD_DIM:(hm + 1) * HEAD_DIM, :] = (o * (1.0 / l)).astype(BF16)


def _proj_body(*refs, rotary_gate):
    it = iter(refs)
    h_ref, g_ref, w_ref, kvm_ref = next(it), next(it), next(it), next(it)
    if rotary_gate:
        pos_ref, freq_ref = next(it), next(it)
    q_ref, k_ref, v_ref, om_ref = next(it), next(it), next(it), next(it)
    if rotary_gate:
        sel_ref = next(it)
    p_scr = next(it)
    if rotary_gate:
        km_scr = next(it)

    u = (_rms_scale(h_ref[0]) * g_ref[...]).astype(BF16)
    p_scr[...] = _dot_nt(w_ref[...], u)

    if rotary_gate:
        own = pl.program_id(1)

        @pl.when(own == 0)
        def _():
            km_scr[...] = jnp.zeros_like(km_scr)

        ang = freq_ref[...] * pos_ref[0].astype(F32)
        n = jnp.floor(ang * _INV_TWO_PI + 0.5)
        red = ((ang - n * _C1) - n * _C2) - n * _C3
        cos, sin = jnp.cos(red), jnp.sin(red)
        for base in range(0, 2 * SELF_WIDTH, HEAD_DIM):
            x1 = p_scr[base:base + ROT_HALF, :]
            x2 = p_scr[base + ROT_HALF:base + 2 * ROT_HALF, :]
            p_scr[base:base + ROT_HALF, :] = x1 * cos - x2 * sin
            p_scr[base + ROT_HALF:base + 2 * ROT_HALF, :] = x2 * cos + x1 * sin

        tm = p_scr.shape[1]
        n_blk = km_scr.shape[1]
        blk_lane = lax.broadcasted_iota(jnp.int32, (HEAD_DIM, n_blk), 1)
        blk = lax.broadcasted_iota(jnp.int32, (n_blk, tm), 0)
        for hd in range(N_SELF_HEADS):
            r0 = hd * HEAD_DIM
            qh = p_scr[r0:r0 + HEAD_DIM, :] * QK_SCALE
            gate = _dot_t(km_scr[r0:r0 + HEAD_DIM, :], qh, precision=lax.Precision.HIGHEST)
            gate = jnp.where(blk < own, gate, -jnp.inf)
            picked = jnp.zeros(gate.shape, jnp.bool_)
            for k in range(MOBA_TOPK):
                top = jnp.max(gate, axis=0, keepdims=True)
                first = jnp.min(jnp.where(gate == top, blk, n_blk), axis=0, keepdims=True)
                hit = blk == first
                picked = jnp.logical_or(picked, jnp.logical_and(hit, k < own))
                gate = jnp.where(hit, -jnp.inf, gate)
            sel_ref[0, hd] = jnp.where(picked, 0.0, NEG_BIAS).astype(BF16)
            kmean = jnp.mean(p_scr[SELF_WIDTH + r0:SELF_WIDTH + r0 + HEAD_DIM, :], axis=1, keepdims=True)
            km_scr[r0:r0 + HEAD_DIM, :] = jnp.where(blk_lane == own, kmean, km_scr[r0:r0 + HEAD_DIM, :])

    q_ref[0] = (p_scr[0:SELF_WIDTH, :] * QK_SCALE).astype(BF16)
    k_ref[0] = p_scr[SELF_WIDTH:2 * SELF_WIDTH, :].astype(BF16)
    v_ref[0] = p_scr[2 * SELF_WIDTH:3 * SELF_WIDTH, :].astype(BF16)
    _memory_attention(p_scr, kvm_ref, om_ref)


def _proj(h, g, w_in_t, kvm, rope=None):
    B, S, D = h.shape
    pw = w_in_t.shape[0]
    tm = MOBA_BLOCK
    assert S % tm == 0
    n_blk = S // tm
    grid = (B, n_blk)
    const2 = lambda b, i: (0, 0)
    in_specs = [pl.BlockSpec((1, tm, D), lambda b, i: (b, i, 0)),
                pl.BlockSpec((1, D), const2),
                pl.BlockSpec(w_in_t.shape, const2, pipeline_mode=pl.Buffered(1)),
                pl.BlockSpec((1,) + kvm.shape[1:], lambda b, i: (b, 0, 0))]
    args = [h, g.reshape(1, D), w_in_t, kvm]
    slab = lambda rows: pl.BlockSpec((1, rows, tm), lambda b, i: (b, 0, i))
    out_shape = [jax.ShapeDtypeStruct((B, SELF_WIDTH, S), BF16)] * 3 + [jax.ShapeDtypeStruct((B, MEM_WIDTH, S), BF16)]
    out_specs = [slab(SELF_WIDTH)] * 3 + [slab(MEM_WIDTH)]
    scratch = [pltpu.VMEM((pw, tm), F32)]
    if rope is not None:
        positions, inv_freq = rope
        in_specs += [pl.BlockSpec((1, 1, tm), lambda b, i: (b, 0, i)),
                     pl.BlockSpec((ROT_HALF, 1), const2)]
        args += [positions.reshape(B, 1, S), inv_freq.reshape(ROT_HALF, 1)]
        out_shape.append(jax.ShapeDtypeStruct((B, N_SELF_HEADS, n_blk, S), BF16))
        out_specs.append(pl.BlockSpec((1, N_SELF_HEADS, n_blk, tm), lambda b, i: (b, 0, 0, i)))
        scratch.append(pltpu.VMEM((SELF_WIDTH, n_blk), F32))
    return pl.pallas_call(
        functools.partial(_proj_body, rotary_gate=rope is not None),
        out_shape=out_shape,
        grid=grid,
        in_specs=in_specs,
        out_specs=out_specs,
        scratch_shapes=scratch,
        compiler_params=pltpu.CompilerParams(dimension_semantics=("arbitrary", "arbitrary"),
                                             vmem_limit_bytes=VMEM_LIMIT),
        name="proj_moba" if rope is not None else "proj_sb",
    )(*args)


def _moba_body(q_ref, k_ref, v_ref, sel_ref, o_ref, bias_scr):
    t = ATT_TILE
    i = pl.program_id(2)
    q = q_ref[0]
    off_i = pl.multiple_of(i * t, t)
    bias_scr[...] = sel_ref[0, 0].astype(F32)

    s = _dot_t(k_ref[0, :, pl.ds(off_i, t)], q)
    key = lax.broadcasted_iota(jnp.int32, (t, t), 0)
    qry = lax.broadcasted_iota(jnp.int32, (t, t), 1)
    s = jnp.where(key <= qry, s, NEG_BIAS)
    m = jnp.max(s, axis=0, keepdims=True)
    p = jnp.exp(s - m)
    l = jnp.sum(p, axis=0, keepdims=True)
    acc = jnp.dot(v_ref[0, :, pl.ds(off_i, t)], p.astype(BF16), preferred_element_type=F32)

    def past_block(j, carry):
        m, l, acc = carry
        off = pl.multiple_of(j * t, t)
        s = _dot_t(k_ref[0, :, pl.ds(off, t)], q)
        bias = bias_scr[pl.ds(j, 1), :]
        m_new = jnp.maximum(m, jnp.max(s, axis=0, keepdims=True) + bias)
        alpha = jnp.exp(m - m_new)
        p = jnp.exp(s + (bias - m_new))
        l = alpha * l + jnp.sum(p, axis=0, keepdims=True)
        acc = alpha * acc + jnp.dot(v_ref[0, :, pl.ds(off, t)], p.astype(BF16), preferred_element_type=F32)
        return m_new, l, acc

    m, l, acc = lax.fori_loop(0, i, past_block, (m, l, acc))
    o_ref[0] = (acc * (1.0 / l)).astype(BF16)


def _moba_attention(q, k, v, sel):
    B, W, S = q.shape
    t = ATT_TILE
    n_blk = S // t
    head_tile = pl.BlockSpec((1, HEAD_DIM, t), lambda b, h, i: (b, h, i))
    head_full = pl.BlockSpec((1, HEAD_DIM, S), lambda b, h, i: (b, h, 0))
    return pl.pallas_call(
        _moba_body,
        out_shape=jax.ShapeDtypeStruct((B, W, S), BF16),
        grid=(B, N_SELF_HEADS, n_blk),
        in_specs=[head_tile, head_full, head_full,
                  pl.BlockSpec((1, 1, n_blk, t), lambda b, h, i: (b, h, 0, i))],
        out_specs=head_tile,
        scratch_shapes=[pltpu.VMEM((n_blk, t), F32)],
        compiler_params=pltpu.CompilerParams(dimension_semantics=("arbitrary",) * 3, vmem_limit_bytes=VMEM_LIMIT),
        name="moba_attn",
    )(q, k, v, sel)


def _sb_tile(kj, vj, q, tri, carry, acc, causal):
    z = _dot_t(kj, q)
    log_1m = -(jnp.maximum(z, 0.0) + jnp.log1p(jnp.exp(-jnp.abs(z))))
    if causal is not None:
        log_1m = jnp.where(causal, log_1m, 0.0)
    hi = log_1m.astype(BF16)
    lo = (log_1m - hi.astype(F32)).astype(BF16)
    incl = jnp.dot(tri, hi, preferred_element_type=F32) + jnp.dot(tri, lo, preferred_element_type=F32)
    a = jnp.exp(z + incl + carry)
    if causal is not None:
        a = jnp.where(causal, a, 0.0)
    acc = acc + jnp.dot(vj, a.astype(BF16), preferred_element_type=F32)
    return carry + incl[0:1, :], acc


def _sb_body(q_ref, k_ref, v_ref, o_ref):
    t = ATT_TILE
    i = pl.program_id(2)
    q = q_ref[0]
    key = lax.broadcasted_iota(jnp.int32, (t, t), 0)
    qry = lax.broadcasted_iota(jnp.int32, (t, t), 1)
    tri = (qry >= key).astype(BF16)
    off_i = pl.multiple_of(i * t, t)
    carry, acc = _sb_tile(k_ref[0, :, pl.ds(off_i, t)], v_ref[0, :, pl.ds(off_i, t)], q, tri,
                          jnp.zeros((1, t), F32), jnp.zeros((HEAD_DIM, t), F32), key < qry)

    def alive(state):
        j, _, _, live = state
        return jnp.logical_and(j >= 0, live > SB_DEAD_LOG)

    def step(state):
        j, carry, acc, _ = state
        off = pl.multiple_of(j * t, t)
        carry, acc = _sb_tile(k_ref[0, :, pl.ds(off, t)], v_ref[0, :, pl.ds(off, t)], q, tri, carry, acc, None)
        return j - 1, carry, acc, jnp.max(carry)

    _, _, acc, _ = lax.while_loop(alive, step, (i - 1, carry, acc, jnp.max(carry)))
    o_ref[0] = acc.astype(BF16)


def _sb_attention(q, k, v):
    B, W, S = q.shape
    t = ATT_TILE
    head_tile = pl.BlockSpec((1, HEAD_DIM, t), lambda b, h, i: (b, h, i))
    head_full = pl.BlockSpec((1, HEAD_DIM, S), lambda b, h, i: (b, h, 0))
    return pl.pallas_call(
        _sb_body,
        out_shape=jax.ShapeDtypeStruct((B, W, S), BF16),
        grid=(B, N_SELF_HEADS, S // t),
        in_specs=[head_tile, head_full, head_full],
        out_specs=head_tile,
        compiler_params=pltpu.CompilerParams(dimension_semantics=("arbitrary",) * 3, vmem_limit_bytes=VMEM_LIMIT),
        name="sb_attn",
    )(q, k, v)


def _chunked_cols(w):
    D, F = w.shape
    return w.astype(BF16).reshape(D, F // FFN_CHUNK, FFN_CHUNK).transpose(1, 0, 2)


def _chunked_rows(w):
    F, D = w.shape
    return w.astype(BF16).reshape(F // FFN_CHUNK, FFN_CHUNK, D)


def kernel(x, mem, positions, g_ffn_pre, w_pre_gate, w_pre_up, w_pre_down, g_mix, w_in, g_mem, w_mem_kv, w_out,
           g_ffn_post, w_post_gate, w_post_up, w_post_down, g_final):
    depth = w_in.shape[0]
    inv_freq = ROPE_THETA ** (-jnp.arange(0, 2 * ROT_HALF, 2, dtype=F32) / (2 * ROT_HALF))
    h = x
    for layer in range(depth):
        h = _ffn(h, g_ffn_pre[layer], _chunked_cols(w_pre_gate[layer]), _chunked_cols(w_pre_up[layer]),
                 _chunked_rows(w_pre_down[layer]))
        kvm = _memkv(mem, g_mem[layer], w_mem_kv[layer].T.astype(BF16))
        w_in_t = w_in[layer].T.astype(BF16)
        if layer % 2 == 0:
            q, k, v, o_mem, sel = _proj(h, g_mix[layer], w_in_t, kvm, rope=(positions, inv_freq))
            o_self = _moba_attention(q, k, v, sel)
        else:
            q, k, v, o_mem = _proj(h, g_mix[layer], w_in_t, kvm)
            o_self = _sb_attention(q, k, v)
        w_o = w_out[layer].astype(BF16)
        h = _ffn(h, g_ffn_post[layer], _chunked_cols(w_post_gate[layer]), _chunked_cols(w_post_up[layer]),
                 _chunked_rows(w_post_down[layer]),
                 attn=(o_self, o_mem, w_o[:SELF_WIDTH], w_o[SELF_WIDTH:]),
                 final_g=g_final if layer == depth - 1 else None)
    return h
```

```python
import functools
import math

import numpy as np
import jax
import jax.numpy as jnp
from jax import lax
from jax.experimental import pallas as pl
from jax.experimental.pallas import tpu as pltpu

F32 = jnp.float32
BF16 = jnp.bfloat16

HEAD_DIM = 64
N_SELF_HEADS = 12
N_MEM_HEADS = 4
SELF_WIDTH = N_SELF_HEADS * HEAD_DIM
MEM_WIDTH = N_MEM_HEADS * HEAD_DIM
ROT_HALF = 8
ROPE_THETA = 500000.0
MOBA_BLOCK = 256
MOBA_TOPK = 3
RMS_EPS = 1e-6
NEG_BIAS = -1e30
QK_SCALE = 1.0 / math.sqrt(HEAD_DIM)
LOG2_E = math.log2(math.e)
ONES_ROWS = 16

FFN_ROWS = 512
FFN_CHUNK = 256
ATT_TILE = 256
MOBA_GROUP = 4
SB_DEAD_LOG = -110.0
VMEM_LIMIT = 56 * 1024 * 1024

_TWO_PI = 2.0 * math.pi
_C1 = 6.28125
_C2 = round((_TWO_PI - _C1) * 2 ** 20) / 2 ** 20
_C3 = float(np.float32(_TWO_PI - _C1 - _C2))
_INV_TWO_PI = float(np.float32(1.0 / _TWO_PI))


def _rms_scale(x):
    return x * lax.rsqrt(jnp.mean(x * x, axis=-1, keepdims=True) + RMS_EPS)


def _dot_t(a, b, **kw):
    return lax.dot_general(a, b, (((0,), (0,)), ((), ())), preferred_element_type=F32, **kw)


def _dot_nt(a, b):
    return lax.dot_general(a, b, (((1,), (1,)), ((), ())), preferred_element_type=F32)


def _ffn_body(*refs, n_chunks, with_attn, with_final):
    it = iter(refs)
    x_ref = next(it)
    if with_attn:
        os_ref, om_ref, wos_ref, wom_ref = next(it), next(it), next(it), next(it)
    g_ref, wg_ref, wu_ref, wd_ref = next(it), next(it), next(it), next(it)
    gf_ref = next(it) if with_final else None
    o_ref, x_scr, u_scr, acc_scr = next(it), next(it), next(it), next(it)

    x = x_ref[...]
    if with_attn:
        x = x + _dot_t(os_ref[0], wos_ref[...]) + _dot_t(om_ref[0], wom_ref[...])
    x_scr[...] = x
    u_scr[...] = (_rms_scale(x) * g_ref[...]).astype(BF16)
    acc_scr[...] = jnp.zeros_like(acc_scr)

    def chunk(c, carry):
        u = u_scr[...]
        gate = jnp.dot(u, wg_ref[c], preferred_element_type=F32)
        up = jnp.dot(u, wu_ref[c], preferred_element_type=F32)
        act = (gate * jax.nn.sigmoid(gate)) * up
        acc_scr[...] += jnp.dot(act.astype(BF16), wd_ref[c], preferred_element_type=F32)
        return carry

    lax.fori_loop(0, n_chunks, chunk, 0)
    y = x_scr[...] + 0.5 * acc_scr[...]
    if with_final:
        y = _rms_scale(y) * gf_ref[...]
    o_ref[...] = y


def _ffn(h, g, wg, wu, wd, attn=None, final_g=None):
    B, S, D = h.shape
    n_chunks, _, chunk_w = wg.shape
    tm = FFN_ROWS
    assert S % tm == 0
    grid = (B, S // tm)
    const2 = lambda b, i: (0, 0)
    const3 = lambda b, i: (0, 0, 0)
    resident = dict(pipeline_mode=pl.Buffered(1))
    in_specs = [pl.BlockSpec((None, tm, D), lambda b, i: (b, i, 0))]
    args = [h]
    if attn is not None:
        o_self, o_mem, w_os, w_om = attn
        in_specs += [
            pl.BlockSpec((1, SELF_WIDTH, tm), lambda b, i: (b, 0, i)),
            pl.BlockSpec((1, MEM_WIDTH, tm), lambda b, i: (b, 0, i)),
            pl.BlockSpec(w_os.shape, const2, **resident),
            pl.BlockSpec(w_om.shape, const2, **resident),
        ]
        args += [o_self, o_mem, w_os, w_om]
    in_specs += [
        pl.BlockSpec((1, D), const2, **resident),
        pl.BlockSpec(wg.shape, const3, **resident),
        pl.BlockSpec(wu.shape, const3, **resident),
        pl.BlockSpec(wd.shape, const3, **resident),
    ]
    args += [g.reshape(1, D), wg, wu, wd]
    if final_g is not None:
        in_specs.append(pl.BlockSpec((1, D), const2, **resident))
        args.append(final_g.reshape(1, D))
    body = functools.partial(_ffn_body, n_chunks=n_chunks, with_attn=attn is not None,
                             with_final=final_g is not None)
    return pl.pallas_call(
        body,
        out_shape=jax.ShapeDtypeStruct((B, S, D), F32),
        grid=grid,
        in_specs=in_specs,
        out_specs=pl.BlockSpec((None, tm, D), lambda b, i: (b, i, 0)),
        scratch_shapes=[pltpu.VMEM((tm, D), F32), pltpu.VMEM((tm, D), BF16), pltpu.VMEM((tm, D), F32)],
        compiler_params=pltpu.CompilerParams(dimension_semantics=("arbitrary", "arbitrary"),
                                             vmem_limit_bytes=VMEM_LIMIT),
        name="ffn_attn" if attn is not None else "ffn",
    )(*args)


def _memkv_body(mem_ref, g_ref, w_ref, o_ref):
    u = (_rms_scale(mem_ref[0]) * g_ref[...]).astype(BF16)
    o_ref[0] = _dot_nt(w_ref[...], u).astype(BF16)


def _memkv(mem, g_mem, w_kv_t):
    B, n_mem, D = mem.shape
    return pl.pallas_call(
        _memkv_body,
        out_shape=jax.ShapeDtypeStruct((B, 2 * MEM_WIDTH, n_mem), BF16),
        grid=(B,),
        in_specs=[pl.BlockSpec((1, n_mem, D), lambda b: (b, 0, 0)),
                  pl.BlockSpec((1, D), lambda b: (0, 0)),
                  pl.BlockSpec(w_kv_t.shape, lambda b: (0, 0))],
        out_specs=pl.BlockSpec((1, 2 * MEM_WIDTH, n_mem), lambda b: (b, 0, 0)),
        name="memkv",
    )(mem, g_mem.reshape(1, -1), w_kv_t)


def _memory_attention(p_scr, kvm_ref, om_ref):
    for hm in range(N_MEM_HEADS):
        r0 = 3 * SELF_WIDTH + hm * HEAD_DIM
        qm = (p_scr[r0:r0 + HEAD_DIM, :] * QK_SCALE).astype(BF16)
        km = kvm_ref[0, hm * HEAD_DIM:(hm + 1) * HEAD_DIM, :]
        vm = kvm_ref[0, MEM_WIDTH + hm * HEAD_DIM:MEM_WIDTH + (hm + 1) * HEAD_DIM, :]
        s = _dot_t(km, qm)
        p = jnp.exp(s - jnp.max(s, axis=0, keepdims=True))
        l = jnp.sum(p, axis=0, keepdims=True)
        o = jnp.dot(vm, p.astype(BF16), preferred_element_type=F32)
        om_ref[0, hm * HEAD_DIM:(hm + 1) * HEAD_DIM, :] = (o * (1.0 / l)).astype(BF16)


def _proj_body(*refs, rotary_gate):
    it = iter(refs)
    h_ref, g_ref, w_ref, kvm_ref = next(it), next(it), next(it), next(it)
    if rotary_gate:
        pos_ref, freq_ref = next(it), next(it)
    q_ref, k_ref, v_ref, om_ref = next(it), next(it), next(it), next(it)
    if rotary_gate:
        sel_ref = next(it)
    p_scr = next(it)
    if rotary_gate:
        km_scr = next(it)

    u = (_rms_scale(h_ref[0]) * g_ref[...]).astype(BF16)
    p_scr[...] = _dot_nt(w_ref[...], u)

    if rotary_gate:
        own = pl.program_id(1)

        @pl.when(own == 0)
        def _():
            km_scr[...] = jnp.zeros_like(km_scr)

        ang = freq_ref[...] * pos_ref[0].astype(F32)
        n = jnp.floor(ang * _INV_TWO_PI + 0.5)
        red = ((ang - n * _C1) - n * _C2) - n * _C3
        cos, sin = jnp.cos(red), jnp.sin(red)
        for base in range(0, 2 * SELF_WIDTH, HEAD_DIM):
            x1 = p_scr[base:base + ROT_HALF, :]
            x2 = p_scr[base + ROT_HALF:base + 2 * ROT_HALF, :]
            p_scr[base:base + ROT_HALF, :] = x1 * cos - x2 * sin
            p_scr[base + ROT_HALF:base + 2 * ROT_HALF, :] = x2 * cos + x1 * sin

        tm = p_scr.shape[1]
        n_blk = km_scr.shape[1]
        blk_lane = lax.broadcasted_iota(jnp.int32, (HEAD_DIM, n_blk), 1)
        blk = lax.broadcasted_iota(jnp.int32, (n_blk, tm), 0)
        for hd in range(N_SELF_HEADS):
            r0 = hd * HEAD_DIM
            qh = p_scr[r0:r0 + HEAD_DIM, :] * QK_SCALE
            gate = _dot_t(km_scr[r0:r0 + HEAD_DIM, :], qh, precision=lax.Precision.HIGHEST)
            gate = jnp.where(blk < own, gate, -jnp.inf)
            picked = jnp.zeros(gate.shape, jnp.bool_)
            for k in range(MOBA_TOPK):
                top = jnp.max(gate, axis=0, keepdims=True)
                first = jnp.min(jnp.where(gate == top, blk, n_blk), axis=0, keepdims=True)
                hit = blk == first
                picked = jnp.logical_or(picked, jnp.logical_and(hit, k < own))
                gate = jnp.where(hit, -jnp.inf, gate)
            sel_ref[0, hd] = jnp.where(picked, 0.0, -jnp.inf).astype(BF16)
            kmean = jnp.mean(p_scr[SELF_WIDTH + r0:SELF_WIDTH + r0 + HEAD_DIM, :], axis=1, keepdims=True)
            km_scr[r0:r0 + HEAD_DIM, :] = jnp.where(blk_lane == own, kmean, km_scr[r0:r0 + HEAD_DIM, :])

    q_ref[0] = (p_scr[0:SELF_WIDTH, :] * (QK_SCALE * LOG2_E if rotary_gate else QK_SCALE)).astype(BF16)
    for hp in range(N_SELF_HEADS // 2):
        r0 = SELF_WIDTH + hp * 2 * HEAD_DIM
        k_ref[0, hp] = p_scr[r0:r0 + 2 * HEAD_DIM, :].T.astype(BF16)
    v_ref[0] = p_scr[2 * SELF_WIDTH:3 * SELF_WIDTH, :].astype(BF16)
    _memory_attention(p_scr, kvm_ref, om_ref)


def _proj(h, g, w_in_t, kvm, rope=None):
    B, S, D = h.shape
    pw = w_in_t.shape[0]
    tm = MOBA_BLOCK
    assert S % tm == 0
    n_blk = S // tm
    grid = (B, n_blk)
    const2 = lambda b, i: (0, 0)
    in_specs = [pl.BlockSpec((1, tm, D), lambda b, i: (b, i, 0)),
                pl.BlockSpec((1, D), const2),
                pl.BlockSpec(w_in_t.shape, const2, pipeline_mode=pl.Buffered(1)),
                pl.BlockSpec((1,) + kvm.shape[1:], lambda b, i: (b, 0, 0))]
    args = [h, g.reshape(1, D), w_in_t, kvm]
    slab = lambda rows: pl.BlockSpec((1, rows, tm), lambda b, i: (b, 0, i))
    n_pairs = N_SELF_HEADS // 2
    out_shape = [jax.ShapeDtypeStruct((B, SELF_WIDTH, S), BF16),
                 jax.ShapeDtypeStruct((B, n_pairs, S, 2 * HEAD_DIM), BF16),
                 jax.ShapeDtypeStruct((B, SELF_WIDTH, S), BF16),
                 jax.ShapeDtypeStruct((B, MEM_WIDTH, S), BF16)]
    out_specs = [slab(SELF_WIDTH),
                 pl.BlockSpec((1, n_pairs, tm, 2 * HEAD_DIM), lambda b, i: (b, 0, i, 0)),
                 slab(SELF_WIDTH), slab(MEM_WIDTH)]
    scratch = [pltpu.VMEM((pw, tm), F32)]
    if rope is not None:
        positions, inv_freq = rope
        in_specs += [pl.BlockSpec((1, 1, tm), lambda b, i: (b, 0, i)),
                     pl.BlockSpec((ROT_HALF, 1), const2)]
        args += [positions.reshape(B, 1, S), inv_freq.reshape(ROT_HALF, 1)]
        out_shape.append(jax.ShapeDtypeStruct((B, N_SELF_HEADS, n_blk, S), BF16))
        out_specs.append(pl.BlockSpec((1, N_SELF_HEADS, n_blk, tm), lambda b, i: (b, 0, 0, i)))
        scratch.append(pltpu.VMEM((SELF_WIDTH, n_blk), F32))
    return pl.pallas_call(
        functools.partial(_proj_body, rotary_gate=rope is not None),
        out_shape=out_shape,
        grid=grid,
        in_specs=in_specs,
        out_specs=out_specs,
        scratch_shapes=scratch,
        compiler_params=pltpu.CompilerParams(dimension_semantics=("arbitrary", "arbitrary"),
                                             vmem_limit_bytes=VMEM_LIMIT),
        name="proj_moba" if rope is not None else "proj_sb",
    )(*args)


def _pad_query(q, parity):
    row = lax.broadcasted_iota(jnp.int32, (2 * HEAD_DIM, q.shape[1]), 0)
    return jnp.where((row >= HEAD_DIM) == (parity == 1), jnp.concatenate([q, q], axis=0), jnp.zeros((), q.dtype))


def _moba_body(q_ref, k_ref, v_ref, sel_ref, o_ref, bias_scr, *s_scr):
    t, grp, d = ATT_TILE, MOBA_GROUP, HEAD_DIM
    i = pl.program_id(2)
    zero = jnp.zeros((d, t), BF16)
    qq = jnp.concatenate([jnp.concatenate([q_ref[0, :d], zero], axis=1),
                          jnp.concatenate([zero, q_ref[0, d:]], axis=1)], axis=0)
    bias_scr[...] = jnp.concatenate([sel_ref[0, 0], sel_ref[0, 1]], axis=1).astype(F32)
    ones = jnp.ones((ONES_ROWS, grp * t), BF16)

    def weighted_values(off, width, p):
        out = []
        for hd in range(2):
            v1 = jnp.concatenate([v_ref[0, hd * d:(hd + 1) * d, pl.ds(off, width)], ones[:, :width]], axis=0)
            out.append(jnp.dot(v1, p[:, hd * t:(hd + 1) * t], preferred_element_type=F32))
        return jnp.concatenate(out, axis=1)

    last_group = bias_scr.shape[0] // grp - 1

    def score_block(jg, g):
        off = pl.multiple_of((jg * grp + g) * t, t)
        s = jnp.dot(k_ref[0, 0, pl.ds(off, t), :], qq, preferred_element_type=F32)
        s_scr[g][...] = s
        return jnp.max(s, axis=0, keepdims=True) + bias_scr[pl.ds(jg * grp + g, 1), :]

    off_i = pl.multiple_of(i * t, t)
    s = jnp.dot(k_ref[0, 0, pl.ds(off_i, t), :], qq, preferred_element_type=F32)
    key = lax.broadcasted_iota(jnp.int32, (t, t), 0)
    qry = lax.broadcasted_iota(jnp.int32, (t, t), 1)
    causal = key <= qry
    s = jnp.where(jnp.concatenate([causal, causal], axis=1), s, -jnp.inf)
    m = jnp.max(s, axis=0, keepdims=True)
    acc = weighted_values(off_i, t, jnp.exp2(s - m).astype(BF16))

    def step(n, carry):
        m, acc, cm = carry
        m_new = jnp.maximum(m, cm)
        nxt = jnp.minimum(n + 1, last_group)
        p, cm_next = [], None
        for g in range(grp):
            p.append(jnp.exp2(s_scr[g][...] + (bias_scr[pl.ds(n * grp + g, 1), :] - m_new)).astype(BF16))
            c = score_block(nxt, g)
            cm_next = c if cm_next is None else jnp.maximum(cm_next, c)
        off = pl.multiple_of(n * (grp * t), grp * t)
        acc = jnp.exp2(m - m_new) * acc + weighted_values(off, grp * t, jnp.concatenate(p, axis=0))
        return m_new, acc, cm_next

    cm = None
    for g in range(grp):
        c = score_block(0, g)
        cm = c if cm is None else jnp.maximum(cm, c)
    m, acc, _ = lax.fori_loop(0, lax.div(i + (grp - 1), grp), step, (m, acc, cm))
    out = acc[:d] * (1.0 / acc[d:d + 1])
    o_ref[0] = jnp.concatenate([out[:, :t], out[:, t:]], axis=0).astype(BF16)


def _moba_attention(q, k, v, sel):
    B, W, S = q.shape
    t = ATT_TILE
    n_blk = S // t
    n_pairs = N_SELF_HEADS // 2
    assert n_blk % MOBA_GROUP == 0
    pair_tile = pl.BlockSpec((1, 2 * HEAD_DIM, t), lambda b, hp, i: (b, hp, i))
    return pl.pallas_call(
        _moba_body,
        out_shape=jax.ShapeDtypeStruct((B, W, S), BF16),
        grid=(B, n_pairs, n_blk),
        in_specs=[pair_tile,
                  pl.BlockSpec((1, 1, S, 2 * HEAD_DIM), lambda b, hp, i: (b, hp, 0, 0)),
                  pl.BlockSpec((1, 2 * HEAD_DIM, S), lambda b, hp, i: (b, hp, 0)),
                  pl.BlockSpec((1, 2, n_blk, t), lambda b, hp, i: (b, hp, 0, i))],
        out_specs=pair_tile,
        scratch_shapes=[pltpu.VMEM((n_blk, 2 * t), F32)] + [pltpu.VMEM((t, 2 * t), F32)] * MOBA_GROUP,
        compiler_params=pltpu.CompilerParams(dimension_semantics=("arbitrary",) * 3, vmem_limit_bytes=VMEM_LIMIT),
        name="moba_attn",
    )(q, k, v, sel)


def _sb_suffix(kj, q, tri2, causal):
    z = jnp.dot(kj, q, preferred_element_type=F32)
    log_1m = -(jnp.maximum(z, 0.0) + jnp.log1p(jnp.exp(-jnp.abs(z))))
    if causal is not None:
        log_1m = jnp.where(causal, log_1m, 0.0)
    hi = log_1m.astype(BF16)
    lo = (log_1m - hi.astype(F32)).astype(BF16)
    return z, jnp.dot(tri2, jnp.concatenate([hi, lo], axis=0), preferred_element_type=F32)


def _sb_body(q_ref, k_ref, v_ref, o_ref):
    t = ATT_TILE
    i = pl.program_id(2)
    q = _pad_query(q_ref[0], lax.rem(pl.program_id(1), 2))
    key = lax.broadcasted_iota(jnp.int32, (t, t), 0)
    qry = lax.broadcasted_iota(jnp.int32, (t, t), 1)
    tri = (qry >= key).astype(BF16)
    tri2 = jnp.concatenate([tri, tri], axis=1)
    causal = key < qry

    off0 = pl.multiple_of(i * t, t)
    off1 = pl.multiple_of(jnp.maximum(i - 1, 0) * t, t)
    z0, incl0 = _sb_suffix(k_ref[0, 0, pl.ds(off0, t), :], q, tri2, causal)
    z1, incl1 = _sb_suffix(k_ref[0, 0, pl.ds(off1, t), :], q, tri2, None)
    has_prev = i >= 1
    a0 = jnp.where(causal, jnp.exp(z0 + incl0), 0.0)
    carry0 = incl0[0:1, :]
    a1 = jnp.exp(z1 + incl1 + (carry0 + jnp.where(has_prev, 0.0, NEG_BIAS)))
    acc = (jnp.dot(v_ref[0, :, pl.ds(off0, t)], a0.astype(BF16), preferred_element_type=F32)
           + jnp.dot(v_ref[0, :, pl.ds(off1, t)], a1.astype(BF16), preferred_element_type=F32))
    carry = carry0 + jnp.where(has_prev, incl1[0:1, :], 0.0)

    def alive(state):
        j, _, _, live = state
        return jnp.logical_and(j >= 0, live > SB_DEAD_LOG)

    def step(state):
        j, carry, acc, _ = state
        off = pl.multiple_of(j * t, t)
        z, incl = _sb_suffix(k_ref[0, 0, pl.ds(off, t), :], q, tri2, None)
        a = jnp.exp(z + incl + carry)
        acc = acc + jnp.dot(v_ref[0, :, pl.ds(off, t)], a.astype(BF16), preferred_element_type=F32)
        carry = carry + incl[0:1, :]
        return j - 1, carry, acc, jnp.max(carry)

    _, _, acc, _ = lax.while_loop(alive, step, (i - 2, carry, acc, jnp.max(carry)))
    o_ref[0] = acc.astype(BF16)


def _sb_attention(q, k, v):
    B, W, S = q.shape
    t = ATT_TILE
    head_tile = pl.BlockSpec((1, HEAD_DIM, t), lambda b, h, i: (b, h, i))
    head_full = pl.BlockSpec((1, HEAD_DIM, S), lambda b, h, i: (b, h, 0))
    return pl.pallas_call(
        _sb_body,
        out_shape=jax.ShapeDtypeStruct((B, W, S), BF16),
        grid=(B, N_SELF_HEADS, S // t),
        in_specs=[head_tile,
                  pl.BlockSpec((1, 1, S, 2 * HEAD_DIM), lambda b, h, i: (b, h // 2, 0, 0)),
                  head_full],
        out_specs=head_tile,
        compiler_params=pltpu.CompilerParams(dimension_semantics=("arbitrary",) * 3, vmem_limit_bytes=VMEM_LIMIT),
        name="sb_attn",
    )(q, k, v)


def _chunked_cols(w):
    D, F = w.shape
    return w.astype(BF16).reshape(D, F // FFN_CHUNK, FFN_CHUNK).transpose(1, 0, 2)


def _chunked_rows(w):
    F, D = w.shape
    return w.astype(BF16).reshape(F // FFN_CHUNK, FFN_CHUNK, D)


def kernel(x, mem, positions, g_ffn_pre, w_pre_gate, w_pre_up, w_pre_down, g_mix, w_in, g_mem, w_mem_kv, w_out,
           g_ffn_post, w_post_gate, w_post_up, w_post_down, g_final):
    depth = w_in.shape[0]
    inv_freq = ROPE_THETA ** (-jnp.arange(0, 2 * ROT_HALF, 2, dtype=F32) / (2 * ROT_HALF))
    h = x
    for layer in range(depth):
        h = _ffn(h, g_ffn_pre[layer], _chunked_cols(w_pre_gate[layer]), _chunked_cols(w_pre_up[layer]),
                 _chunked_rows(w_pre_down[layer]))
        kvm = _memkv(mem, g_mem[layer], w_mem_kv[layer].T.astype(BF16))
        w_in_t = w_in[layer].T.astype(BF16)
        if layer % 2 == 0:
            q, k, v, o_mem, sel = _proj(h, g_mix[layer], w_in_t, kvm, rope=(positions, inv_freq))
            o_self = _moba_attention(q, k, v, sel)
        else:
            q, k, v, o_mem = _proj(h, g_mix[layer], w_in_t, kvm)
            o_self = _sb_attention(q, k, v)
        w_o = w_out[layer].astype(BF16)
        h = _ffn(h, g_ffn_post[layer], _chunked_cols(w_post_gate[layer]), _chunked_cols(w_post_up[layer]),
                 _chunked_rows(w_post_down[layer]),
                 attn=(o_self, o_mem, w_o[:SELF_WIDTH], w_o[SELF_WIDTH:]),
                 final_g=g_final if layer == depth - 1 else None)
    return h
```

```python
import functools
import math

import numpy as np
import jax
import jax.numpy as jnp
from jax import lax
from jax.experimental import pallas as pl
from jax.experimental.pallas import tpu as pltpu

F32 = jnp.float32
BF16 = jnp.bfloat16

HEAD_DIM = 64
N_SELF_HEADS = 12
N_MEM_HEADS = 4
SELF_WIDTH = N_SELF_HEADS * HEAD_DIM
MEM_WIDTH = N_MEM_HEADS * HEAD_DIM
ROT_HALF = 8
ROPE_THETA = 500000.0
MOBA_BLOCK = 256
MOBA_TOPK = 3
RMS_EPS = 1e-6
NEG_BIAS = -1e30
QK_SCALE = 1.0 / math.sqrt(HEAD_DIM)
LOG2_E = math.log2(math.e)
ONES_ROWS = 16

FFN_ROWS = 512
FFN_CHUNK = 256
ATT_TILE = 256
MOBA_GROUP = 4
MOBA_PAIRS = 3
SB_PAIRS = 3
SB_DEAD_LOG2 = -160.0
VMEM_LIMIT = 56 * 1024 * 1024

_TWO_PI = 2.0 * math.pi
_C1 = 6.28125
_C2 = round((_TWO_PI - _C1) * 2 ** 20) / 2 ** 20
_C3 = float(np.float32(_TWO_PI - _C1 - _C2))
_INV_TWO_PI = float(np.float32(1.0 / _TWO_PI))


def _rms_scale(x):
    return x * lax.rsqrt(jnp.mean(x * x, axis=-1, keepdims=True) + RMS_EPS)


def _dot_t(a, b, **kw):
    return lax.dot_general(a, b, (((0,), (0,)), ((), ())), preferred_element_type=F32, **kw)


def _dot_nt(a, b):
    return lax.dot_general(a, b, (((1,), (1,)), ((), ())), preferred_element_type=F32)


def _ffn_body(*refs, n_chunks, with_attn, with_final):
    it = iter(refs)
    x_ref = next(it)
    if with_attn:
        os_ref, om_ref, wos_ref, wom_ref = next(it), next(it), next(it), next(it)
    g_ref, wg_ref, wu_ref, wd_ref = next(it), next(it), next(it), next(it)
    gf_ref = next(it) if with_final else None
    o_ref, x_scr, u_scr, acc_scr = next(it), next(it), next(it), next(it)

    x = x_ref[...]
    if with_attn:
        x = x + _dot_t(os_ref[0], wos_ref[...]) + _dot_t(om_ref[0], wom_ref[...])
    x_scr[...] = x
    u_scr[...] = (_rms_scale(x) * g_ref[...]).astype(BF16)
    acc_scr[...] = jnp.zeros_like(acc_scr)

    def chunk(c, carry):
        u = u_scr[...]
        gate = jnp.dot(u, wg_ref[c], preferred_element_type=F32)
        up = jnp.dot(u, wu_ref[c], preferred_element_type=F32)
        act = (gate * jax.nn.sigmoid(gate)) * up
        acc_scr[...] += jnp.dot(act.astype(BF16), wd_ref[c], preferred_element_type=F32)
        return carry

    lax.fori_loop(0, n_chunks, chunk, 0)
    y = x_scr[...] + 0.5 * acc_scr[...]
    if with_final:
        y = _rms_scale(y) * gf_ref[...]
    o_ref[...] = y


def _ffn(h, g, wg, wu, wd, attn=None, final_g=None):
    B, S, D = h.shape
    n_chunks, _, chunk_w = wg.shape
    tm = FFN_ROWS
    assert S % tm == 0
    grid = (B, S // tm)
    const2 = lambda b, i: (0, 0)
    const3 = lambda b, i: (0, 0, 0)
    resident = dict(pipeline_mode=pl.Buffered(1))
    in_specs = [pl.BlockSpec((None, tm, D), lambda b, i: (b, i, 0))]
    args = [h]
    if attn is not None:
        o_self, o_mem, w_os, w_om = attn
        in_specs += [
            pl.BlockSpec((1, SELF_WIDTH, tm), lambda b, i: (b, 0, i)),
            pl.BlockSpec((1, MEM_WIDTH, tm), lambda b, i: (b, 0, i)),
            pl.BlockSpec(w_os.shape, const2, **resident),
            pl.BlockSpec(w_om.shape, const2, **resident),
        ]
        args += [o_self, o_mem, w_os, w_om]
    in_specs += [
        pl.BlockSpec((1, D), const2, **resident),
        pl.BlockSpec(wg.shape, const3, **resident),
        pl.BlockSpec(wu.shape, const3, **resident),
        pl.BlockSpec(wd.shape, const3, **resident),
    ]
    args += [g.reshape(1, D), wg, wu, wd]
    if final_g is not None:
        in_specs.append(pl.BlockSpec((1, D), const2, **resident))
        args.append(final_g.reshape(1, D))
    body = functools.partial(_ffn_body, n_chunks=n_chunks, with_attn=attn is not None,
                             with_final=final_g is not None)
    return pl.pallas_call(
        body,
        out_shape=jax.ShapeDtypeStruct((B, S, D), F32),
        grid=grid,
        in_specs=in_specs,
        out_specs=pl.BlockSpec((None, tm, D), lambda b, i: (b, i, 0)),
        scratch_shapes=[pltpu.VMEM((tm, D), F32), pltpu.VMEM((tm, D), BF16), pltpu.VMEM((tm, D), F32)],
        compiler_params=pltpu.CompilerParams(dimension_semantics=("arbitrary", "arbitrary"),
                                             vmem_limit_bytes=VMEM_LIMIT),
        name="ffn_attn" if attn is not None else "ffn",
    )(*args)


def _memkv_body(mem_ref, g_ref, w_ref, o_ref):
    u = (_rms_scale(mem_ref[0]) * g_ref[...]).astype(BF16)
    o_ref[0] = _dot_nt(w_ref[...], u).astype(BF16)


def _memkv(mem, g_mem, w_kv_t):
    B, n_mem, D = mem.shape
    return pl.pallas_call(
        _memkv_body,
        out_shape=jax.ShapeDtypeStruct((B, 2 * MEM_WIDTH, n_mem), BF16),
        grid=(B,),
        in_specs=[pl.BlockSpec((1, n_mem, D), lambda b: (b, 0, 0)),
                  pl.BlockSpec((1, D), lambda b: (0, 0)),
                  pl.BlockSpec(w_kv_t.shape, lambda b: (0, 0))],
        out_specs=pl.BlockSpec((1, 2 * MEM_WIDTH, n_mem), lambda b: (b, 0, 0)),
        name="memkv",
    )(mem, g_mem.reshape(1, -1), w_kv_t)


def _memory_attention(p_scr, kvm_ref, om_ref):
    for hm in range(N_MEM_HEADS):
        r0 = 3 * SELF_WIDTH + hm * HEAD_DIM
        qm = (p_scr[r0:r0 + HEAD_DIM, :] * QK_SCALE).astype(BF16)
        km = kvm_ref[0, hm * HEAD_DIM:(hm + 1) * HEAD_DIM, :]
        vm = kvm_ref[0, MEM_WIDTH + hm * HEAD_DIM:MEM_WIDTH + (hm + 1) * HEAD_DIM, :]
        s = _dot_t(km, qm)
        p = jnp.exp(s - jnp.max(s, axis=0, keepdims=True))
        l = jnp.sum(p, axis=0, keepdims=True)
        o = jnp.dot(vm, p.astype(BF16), preferred_element_type=F32)
        om_ref[0, hm * HEAD_DIM:(hm + 1) * HEAD_DIM, :] = (o * (1.0 / l)).astype(BF16)


def _proj_body(*refs, rotary_gate):
    it = iter(refs)
    h_ref, g_ref, w_ref, kvm_ref = next(it), next(it), next(it), next(it)
    if rotary_gate:
        pos_ref, freq_ref = next(it), next(it)
    q_ref, k_ref, v_ref, om_ref = next(it), next(it), next(it), next(it)
    if rotary_gate:
        sel_ref = next(it)
    p_scr = next(it)
    if rotary_gate:
        km_scr = next(it)

    u = (_rms_scale(h_ref[0]) * g_ref[...]).astype(BF16)
    p_scr[...] = _dot_nt(w_ref[...], u)

    if rotary_gate:
        own = pl.program_id(1)

        @pl.when(own == 0)
        def _():
            km_scr[...] = jnp.zeros_like(km_scr)

        ang = freq_ref[...] * pos_ref[0].astype(F32)
        n = jnp.floor(ang * _INV_TWO_PI + 0.5)
        red = ((ang - n * _C1) - n * _C2) - n * _C3
        cos, sin = jnp.cos(red), jnp.sin(red)
        for base in range(0, 2 * SELF_WIDTH, HEAD_DIM):
            x1 = p_scr[base:base + ROT_HALF, :]
            x2 = p_scr[base + ROT_HALF:base + 2 * ROT_HALF, :]
            p_scr[base:base + ROT_HALF, :] = x1 * cos - x2 * sin
            p_scr[base + ROT_HALF:base + 2 * ROT_HALF, :] = x2 * cos + x1 * sin

        tm = p_scr.shape[1]
        n_blk = km_scr.shape[1]
        blk_lane = lax.broadcasted_iota(jnp.int32, (HEAD_DIM, n_blk), 1)
        blk = lax.broadcasted_iota(jnp.int32, (n_blk, tm), 0)
        for hd in range(N_SELF_HEADS):
            r0 = hd * HEAD_DIM
            qh = p_scr[r0:r0 + HEAD_DIM, :] * QK_SCALE
            gate = _dot_t(km_scr[r0:r0 + HEAD_DIM, :], qh, precision=lax.Precision.HIGHEST)
            gate = jnp.where(blk < own, gate, -jnp.inf)
            picked = jnp.zeros(gate.shape, jnp.bool_)
            for k in range(MOBA_TOPK):
                top = jnp.max(gate, axis=0, keepdims=True)
                first = jnp.min(jnp.where(gate == top, blk, n_blk), axis=0, keepdims=True)
                hit = blk == first
                picked = jnp.logical_or(picked, jnp.logical_and(hit, k < own))
                gate = jnp.where(hit, -jnp.inf, gate)
            sel_ref[0, hd] = jnp.where(picked, 0.0, -jnp.inf).astype(BF16)
            kmean = jnp.mean(p_scr[SELF_WIDTH + r0:SELF_WIDTH + r0 + HEAD_DIM, :], axis=1, keepdims=True)
            km_scr[r0:r0 + HEAD_DIM, :] = jnp.where(blk_lane == own, kmean, km_scr[r0:r0 + HEAD_DIM, :])

    q_ref[0] = (p_scr[0:SELF_WIDTH, :] * (QK_SCALE * LOG2_E if rotary_gate else -QK_SCALE * LOG2_E)).astype(BF16)
    for hp in range(N_SELF_HEADS // 2):
        r0 = SELF_WIDTH + hp * 2 * HEAD_DIM
        k_ref[0, hp] = p_scr[r0:r0 + 2 * HEAD_DIM, :].T.astype(BF16)
    v_ref[0] = p_scr[2 * SELF_WIDTH:3 * SELF_WIDTH, :].astype(BF16)
    _memory_attention(p_scr, kvm_ref, om_ref)


def _proj(h, g, w_in_t, kvm, rope=None):
    B, S, D = h.shape
    pw = w_in_t.shape[0]
    tm = MOBA_BLOCK
    assert S % tm == 0
    n_blk = S // tm
    grid = (B, n_blk)
    const2 = lambda b, i: (0, 0)
    in_specs = [pl.BlockSpec((1, tm, D), lambda b, i: (b, i, 0)),
                pl.BlockSpec((1, D), const2),
                pl.BlockSpec(w_in_t.shape, const2, pipeline_mode=pl.Buffered(1)),
                pl.BlockSpec((1,) + kvm.shape[1:], lambda b, i: (b, 0, 0))]
    args = [h, g.reshape(1, D), w_in_t, kvm]
    slab = lambda rows: pl.BlockSpec((1, rows, tm), lambda b, i: (b, 0, i))
    n_pairs = N_SELF_HEADS // 2
    out_shape = [jax.ShapeDtypeStruct((B, SELF_WIDTH, S), BF16),
                 jax.ShapeDtypeStruct((B, n_pairs, S, 2 * HEAD_DIM), BF16),
                 jax.ShapeDtypeStruct((B, SELF_WIDTH, S), BF16),
                 jax.ShapeDtypeStruct((B, MEM_WIDTH, S), BF16)]
    out_specs = [slab(SELF_WIDTH),
                 pl.BlockSpec((1, n_pairs, tm, 2 * HEAD_DIM), lambda b, i: (b, 0, i, 0)),
                 slab(SELF_WIDTH), slab(MEM_WIDTH)]
    scratch = [pltpu.VMEM((pw, tm), F32)]
    if rope is not None:
        positions, inv_freq = rope
        in_specs += [pl.BlockSpec((1, 1, tm), lambda b, i: (b, 0, i)),
                     pl.BlockSpec((ROT_HALF, 1), const2)]
        args += [positions.reshape(B, 1, S), inv_freq.reshape(ROT_HALF, 1)]
        out_shape.append(jax.ShapeDtypeStruct((B, N_SELF_HEADS, n_blk, S), BF16))
        out_specs.append(pl.BlockSpec((1, N_SELF_HEADS, n_blk, tm), lambda b, i: (b, 0, 0, i)))
        scratch.append(pltpu.VMEM((SELF_WIDTH, n_blk), F32))
    return pl.pallas_call(
        functools.partial(_proj_body, rotary_gate=rope is not None),
        out_shape=out_shape,
        grid=grid,
        in_specs=in_specs,
        out_specs=out_specs,
        scratch_shapes=scratch,
        compiler_params=pltpu.CompilerParams(dimension_semantics=("arbitrary", "arbitrary"),
                                             vmem_limit_bytes=VMEM_LIMIT),
        name="proj_moba" if rope is not None else "proj_sb",
    )(*args)


def _pair_queries(q):
    d, t = HEAD_DIM, q.shape[1]
    zero = jnp.zeros((d, t), q.dtype)
    return jnp.concatenate([jnp.concatenate([q[:d], zero], axis=1), jnp.concatenate([zero, q[d:]], axis=1)], axis=0)


def _moba_body(q_ref, k_ref, v_ref, sel_ref, o_ref, bias_scr, *s_scr):
    t, grp, d, pairs = ATT_TILE, MOBA_GROUP, HEAD_DIM, range(MOBA_PAIRS)
    i = pl.program_id(2)
    qq = [_pair_queries(q_ref[0, 2 * d * pr:2 * d * (pr + 1)]) for pr in pairs]
    bias_scr[...] = jnp.concatenate([sel_ref[0, hd] for hd in range(2 * MOBA_PAIRS)], axis=1).astype(F32)
    ones = jnp.ones((ONES_ROWS, t), BF16)
    last_group = bias_scr.shape[0] // grp - 1

    def bias(j, pr):
        return bias_scr[pl.ds(j, 1), 2 * t * pr:2 * t * (pr + 1)]

    def weighted_values(pr, off, p):
        out = []
        for hd in range(2):
            r0 = (2 * pr + hd) * d
            v1 = jnp.concatenate([v_ref[0, r0:r0 + d, pl.ds(off, t)], ones], axis=0)
            out.append(jnp.dot(v1, p[:, hd * t:(hd + 1) * t], preferred_element_type=F32))
        return jnp.concatenate(out, axis=1)

    def score_block(jg, g, pr):
        off = pl.multiple_of((jg * grp + g) * t, t)
        s = jnp.dot(k_ref[0, pr, pl.ds(off, t), :], qq[pr], preferred_element_type=F32)
        s_scr[g * MOBA_PAIRS + pr][...] = s
        return jnp.max(s, axis=0, keepdims=True) + bias(jg * grp + g, pr)

    def score_group(jg):
        cm = [None] * MOBA_PAIRS
        for g in range(grp):
            for pr in pairs:
                c = score_block(jg, g, pr)
                cm[pr] = c if cm[pr] is None else jnp.maximum(cm[pr], c)
        return tuple(cm)

    off_i = pl.multiple_of(i * t, t)
    key = lax.broadcasted_iota(jnp.int32, (t, t), 0)
    qry = lax.broadcasted_iota(jnp.int32, (t, t), 1)
    causal = jnp.concatenate([key <= qry] * 2, axis=1)
    m, acc = [], []
    for pr in pairs:
        s = jnp.dot(k_ref[0, pr, pl.ds(off_i, t), :], qq[pr], preferred_element_type=F32)
        s = jnp.where(causal, s, -jnp.inf)
        m.append(jnp.max(s, axis=0, keepdims=True))
        acc.append(weighted_values(pr, off_i, jnp.exp2(s - m[pr]).astype(BF16)))

    def step(n, carry):
        m, acc, cm = carry
        m_new = [jnp.maximum(m[pr], cm[pr]) for pr in pairs]
        nxt = jnp.minimum(n + 1, last_group)
        pv, cm_next = [None] * MOBA_PAIRS, [None] * MOBA_PAIRS
        for g in range(grp):
            off = pl.multiple_of((n * grp + g) * t, t)
            for pr in pairs:
                s = s_scr[g * MOBA_PAIRS + pr][...]
                p = jnp.exp2(s + (bias(n * grp + g, pr) - m_new[pr])).astype(BF16)
                w = weighted_values(pr, off, p)
                pv[pr] = w if pv[pr] is None else pv[pr] + w
                c = score_block(nxt, g, pr)
                cm_next[pr] = c if cm_next[pr] is None else jnp.maximum(cm_next[pr], c)
        acc = [jnp.exp2(m[pr] - m_new[pr]) * acc[pr] + pv[pr] for pr in pairs]
        return tuple(m_new), tuple(acc), tuple(cm_next)

    carry = (tuple(m), tuple(acc), score_group(0))
    _, acc, _ = lax.fori_loop(0, lax.div(i + (grp - 1), grp), step, carry)
    for pr in pairs:
        out = acc[pr][:d] * (1.0 / acc[pr][d:d + 1])
        o_ref[0, 2 * d * pr:2 * d * (pr + 1)] = jnp.concatenate([out[:, :t], out[:, t:]], axis=0).astype(BF16)


def _moba_attention(q, k, v, sel):
    B, W, S = q.shape
    t = ATT_TILE
    n_blk = S // t
    assert n_blk % MOBA_GROUP == 0 and N_SELF_HEADS % (2 * MOBA_PAIRS) == 0
    rows = 2 * HEAD_DIM * MOBA_PAIRS
    q_tile = pl.BlockSpec((1, rows, t), lambda b, hq, i: (b, hq, i))
    resident = dict(pipeline_mode=pl.Buffered(1))
    return pl.pallas_call(
        _moba_body,
        out_shape=jax.ShapeDtypeStruct((B, W, S), BF16),
        grid=(B, N_SELF_HEADS // (2 * MOBA_PAIRS), n_blk),
        in_specs=[q_tile,
                  pl.BlockSpec((1, MOBA_PAIRS, S, 2 * HEAD_DIM), lambda b, hq, i: (b, hq, 0, 0), **resident),
                  pl.BlockSpec((1, rows, S), lambda b, hq, i: (b, hq, 0), **resident),
                  pl.BlockSpec((1, 2 * MOBA_PAIRS, n_blk, t), lambda b, hq, i: (b, hq, 0, i))],
        out_specs=q_tile,
        scratch_shapes=([pltpu.VMEM((n_blk, 2 * t * MOBA_PAIRS), F32)]
                        + [pltpu.VMEM((t, 2 * t), F32)] * (MOBA_GROUP * MOBA_PAIRS)),
        compiler_params=pltpu.CompilerParams(dimension_semantics=("arbitrary",) * 3, vmem_limit_bytes=VMEM_LIMIT),
        name="moba_attn",
    )(q, k, v, sel)


def _sb_body(q_ref, k_ref, v_ref, o_ref):
    t, d, pairs = ATT_TILE, HEAD_DIM, range(SB_PAIRS)
    i = pl.program_id(2)
    qq = [_pair_queries(q_ref[0, 2 * d * pr:2 * d * (pr + 1)]) for pr in pairs]
    key = lax.broadcasted_iota(jnp.int32, (t, t), 0)
    qry = lax.broadcasted_iota(jnp.int32, (t, t), 1)
    tri = (qry >= key).astype(BF16)
    tri2 = jnp.concatenate([tri, tri], axis=1)
    causal = jnp.concatenate([key < qry] * 2, axis=1)

    def suffix(pr, off, masked):
        zn = jnp.dot(k_ref[0, pr, pl.ds(off, t), :], qq[pr], preferred_element_type=F32)
        log_1m = jnp.minimum(zn, 0.0) - jnp.log(1.0 + jnp.exp2(-jnp.abs(zn))) * LOG2_E
        if masked:
            log_1m = jnp.where(causal, log_1m, 0.0)
        hi = log_1m.astype(BF16)
        lo = (log_1m - hi.astype(F32)).astype(BF16)
        return zn, jnp.dot(tri2, jnp.concatenate([hi, lo], axis=0), preferred_element_type=F32)

    def weighted_values(pr, off, a):
        out = []
        for hd in range(2):
            r0 = (2 * pr + hd) * d
            out.append(jnp.dot(v_ref[0, r0:r0 + d, pl.ds(off, t)], a[:, hd * t:(hd + 1) * t],
                               preferred_element_type=F32))
        return jnp.concatenate(out, axis=1)

    off0 = pl.multiple_of(i * t, t)
    off1 = pl.multiple_of(jnp.maximum(i - 1, 0) * t, t)
    has_prev = i >= 1
    carry, acc = [], []
    for pr in pairs:
        z0, incl0 = suffix(pr, off0, True)
        z1, incl1 = suffix(pr, off1, False)
        a0 = jnp.where(causal, jnp.exp2(incl0 - z0), 0.0)
        carry0 = incl0[0:1, :]
        a1 = jnp.exp2((incl1 + (carry0 + jnp.where(has_prev, 0.0, NEG_BIAS))) - z1)
        acc.append(weighted_values(pr, off0, a0.astype(BF16)) + weighted_values(pr, off1, a1.astype(BF16)))
        carry.append(carry0 + jnp.where(has_prev, incl1[0:1, :], 0.0))

    def liveness(carry):
        live = jnp.max(carry[0])
        for pr in pairs[1:]:
            live = jnp.maximum(live, jnp.max(carry[pr]))
        return live

    def alive(state):
        j, _, _, live = state
        return jnp.logical_and(j >= 0, live > SB_DEAD_LOG2)

    def step(state):
        j, carry, acc, _ = state
        off = pl.multiple_of(j * t, t)
        carry, acc = list(carry), list(acc)
        for pr in pairs:
            z, incl = suffix(pr, off, False)
            acc[pr] = acc[pr] + weighted_values(pr, off, jnp.exp2((incl + carry[pr]) - z).astype(BF16))
            carry[pr] = carry[pr] + incl[0:1, :]
        return j - 1, tuple(carry), tuple(acc), liveness(carry)

    _, _, acc, _ = lax.while_loop(alive, step, (i - 2, tuple(carry), tuple(acc), liveness(carry)))
    for pr in pairs:
        o_ref[0, 2 * d * pr:2 * d * (pr + 1)] = jnp.concatenate([acc[pr][:, :t], acc[pr][:, t:]], axis=0).astype(BF16)


def _sb_attention(q, k, v):
    B, W, S = q.shape
    t = ATT_TILE
    assert N_SELF_HEADS % (2 * SB_PAIRS) == 0
    rows = 2 * HEAD_DIM * SB_PAIRS
    q_tile = pl.BlockSpec((1, rows, t), lambda b, hq, i: (b, hq, i))
    resident = dict(pipeline_mode=pl.Buffered(1))
    return pl.pallas_call(
        _sb_body,
        out_shape=jax.ShapeDtypeStruct((B, W, S), BF16),
        grid=(B, N_SELF_HEADS // (2 * SB_PAIRS), S // t),
        in_specs=[q_tile,
                  pl.BlockSpec((1, SB_PAIRS, S, 2 * HEAD_DIM), lambda b, hq, i: (b, hq, 0, 0), **resident),
                  pl.BlockSpec((1, rows, S), lambda b, hq, i: (b, hq, 0), **resident)],
        out_specs=q_tile,
        compiler_params=pltpu.CompilerParams(dimension_semantics=("arbitrary",) * 3, vmem_limit_bytes=VMEM_LIMIT),
        name="sb_attn",
    )(q, k, v)


def _chunked_cols(w):
    D, F = w.shape
    return w.astype(BF16).reshape(D, F // FFN_CHUNK, FFN_CHUNK).transpose(1, 0, 2)


def _chunked_rows(w):
    F, D = w.shape
    return w.astype(BF16).reshape(F // FFN_CHUNK, FFN_CHUNK, D)


def kernel(x, mem, positions, g_ffn_pre, w_pre_gate, w_pre_up, w_pre_down, g_mix, w_in, g_mem, w_mem_kv, w_out,
           g_ffn_post, w_post_gate, w_post_up, w_post_down, g_final):
    depth = w_in.shape[0]
    inv_freq = ROPE_THETA ** (-jnp.arange(0, 2 * ROT_HALF, 2, dtype=F32) / (2 * ROT_HALF))
    h = x
    for layer in range(depth):
        h = _ffn(h, g_ffn_pre[layer], _chunked_cols(w_pre_gate[layer]), _chunked_cols(w_pre_up[layer]),
                 _chunked_rows(w_pre_down[layer]))
        kvm = _memkv(mem, g_mem[layer], w_mem_kv[layer].T.astype(BF16))
        w_in_t = w_in[layer].T.astype(BF16)
        if layer % 2 == 0:
            q, k, v, o_mem, sel = _proj(h, g_mix[layer], w_in_t, kvm, rope=(positions, inv_freq))
            o_self = _moba_attention(q, k, v, sel)
        else:
            q, k, v, o_mem = _proj(h, g_mix[layer], w_in_t, kvm)
            o_self = _sb_attention(q, k, v)
        w_o = w_out[layer].astype(BF16)
        h = _ffn(h, g_ffn_post[layer], _chunked_cols(w_post_gate[layer]), _chunked_cols(w_post_up[layer]),
                 _chunked_rows(w_post_down[layer]),
                 attn=(o_self, o_mem, w_o[:SELF_WIDTH], w_o[SELF_WIDTH:]),
                 final_g=g_final if layer == depth - 1 else None)
    return h
```

```python
import functools
import math

import numpy as np
import jax
import jax.numpy as jnp
from jax import lax
from jax.experimental import pallas as pl
from jax.experimental.pallas import tpu as pltpu

F32 = jnp.float32
BF16 = jnp.bfloat16

HEAD_DIM = 64
N_SELF_HEADS = 12
N_MEM_HEADS = 4
SELF_WIDTH = N_SELF_HEADS * HEAD_DIM
MEM_WIDTH = N_MEM_HEADS * HEAD_DIM
ROT_HALF = 8
ROPE_THETA = 500000.0
MOBA_BLOCK = 256
MOBA_TOPK = 3
RMS_EPS = 1e-6
NEG_BIAS = -1e30
QK_SCALE = 1.0 / math.sqrt(HEAD_DIM)
LOG2_E = math.log2(math.e)
ONES_ROWS = 16

FFN_ROWS = 512
FFN_CHUNK = 256
PROJ_SUBTILES = 2
ATT_TILE = 256
MOBA_GROUP = 4
MOBA_PAIRS = 3
SB_PAIRS = 3
SB_DEAD_LOG2 = -160.0
VMEM_LIMIT = 56 * 1024 * 1024

_TWO_PI = 2.0 * math.pi
_C1 = 6.28125
_C2 = round((_TWO_PI - _C1) * 2 ** 20) / 2 ** 20
_C3 = float(np.float32(_TWO_PI - _C1 - _C2))
_INV_TWO_PI = float(np.float32(1.0 / _TWO_PI))


def _rms_scale(x):
    return x * lax.rsqrt(jnp.mean(x * x, axis=-1, keepdims=True) + RMS_EPS)


def _dot_t(a, b, **kw):
    return lax.dot_general(a, b, (((0,), (0,)), ((), ())), preferred_element_type=F32, **kw)


def _dot_nt(a, b):
    return lax.dot_general(a, b, (((1,), (1,)), ((), ())), preferred_element_type=F32)


def _ffn_body(*refs, n_chunks, with_attn, with_final):
    it = iter(refs)
    x_ref = next(it)
    if with_attn:
        os_ref, om_ref, wos_ref, wom_ref = next(it), next(it), next(it), next(it)
    g_ref, wg_ref, wu_ref, wd_ref = next(it), next(it), next(it), next(it)
    gf_ref = next(it) if with_final else None
    o_ref, x_scr, u_scr, acc_scr = next(it), next(it), next(it), next(it)

    x = x_ref[...]
    if with_attn:
        x = x + _dot_t(os_ref[0], wos_ref[...]) + _dot_t(om_ref[0], wom_ref[...])
    x_scr[...] = x
    u_scr[...] = (_rms_scale(x) * g_ref[...]).astype(BF16)
    acc_scr[...] = jnp.zeros_like(acc_scr)

    for c in range(n_chunks):
        cols = slice(c * FFN_CHUNK, (c + 1) * FFN_CHUNK)
        u = u_scr[...]
        gate = jnp.dot(u, wg_ref[:, cols], preferred_element_type=F32)
        up = jnp.dot(u, wu_ref[:, cols], preferred_element_type=F32)
        act = (gate * jax.nn.sigmoid(gate)) * up
        acc_scr[...] += jnp.dot(act.astype(BF16), wd_ref[cols, :], preferred_element_type=F32)
    y = x_scr[...] + 0.5 * acc_scr[...]
    if with_final:
        y = _rms_scale(y) * gf_ref[...]
    o_ref[...] = y


def _ffn(h, g, wg, wu, wd, attn=None, final_g=None):
    B, S, D = h.shape
    assert wg.shape[1] % FFN_CHUNK == 0
    n_chunks = wg.shape[1] // FFN_CHUNK
    tm = FFN_ROWS
    assert S % tm == 0
    grid = (B, S // tm)
    const2 = lambda b, i: (0, 0)
    resident = dict(pipeline_mode=pl.Buffered(1))
    in_specs = [pl.BlockSpec((None, tm, D), lambda b, i: (b, i, 0))]
    args = [h]
    if attn is not None:
        o_self, o_mem, w_os, w_om = attn
        in_specs += [
            pl.BlockSpec((1, SELF_WIDTH, tm), lambda b, i: (b, 0, i)),
            pl.BlockSpec((1, MEM_WIDTH, tm), lambda b, i: (b, 0, i)),
            pl.BlockSpec(w_os.shape, const2, **resident),
            pl.BlockSpec(w_om.shape, const2, **resident),
        ]
        args += [o_self, o_mem, w_os, w_om]
    in_specs += [
        pl.BlockSpec((1, D), const2, **resident),
        pl.BlockSpec(wg.shape, const2, **resident),
        pl.BlockSpec(wu.shape, const2, **resident),
        pl.BlockSpec(wd.shape, const2, **resident),
    ]
    args += [g.reshape(1, D), wg, wu, wd]
    if final_g is not None:
        in_specs.append(pl.BlockSpec((1, D), const2, **resident))
        args.append(final_g.reshape(1, D))
    body = functools.partial(_ffn_body, n_chunks=n_chunks, with_attn=attn is not None,
                             with_final=final_g is not None)
    return pl.pallas_call(
        body,
        out_shape=jax.ShapeDtypeStruct((B, S, D), F32),
        grid=grid,
        in_specs=in_specs,
        out_specs=pl.BlockSpec((None, tm, D), lambda b, i: (b, i, 0)),
        scratch_shapes=[pltpu.VMEM((tm, D), F32), pltpu.VMEM((tm, D), BF16), pltpu.VMEM((tm, D), F32)],
        compiler_params=pltpu.CompilerParams(dimension_semantics=("arbitrary", "arbitrary"),
                                             vmem_limit_bytes=VMEM_LIMIT),
        name="ffn_attn" if attn is not None else "ffn",
    )(*args)


def _memkv_body(mem_ref, g_ref, w_ref, o_ref):
    u = (_rms_scale(mem_ref[0]) * g_ref[...]).astype(BF16)
    o_ref[0] = _dot_nt(w_ref[...], u).astype(BF16)


def _memkv(mem, g_mem, w_kv_t):
    B, n_mem, D = mem.shape
    return pl.pallas_call(
        _memkv_body,
        out_shape=jax.ShapeDtypeStruct((B, 2 * MEM_WIDTH, n_mem), BF16),
        grid=(B,),
        in_specs=[pl.BlockSpec((1, n_mem, D), lambda b: (b, 0, 0)),
                  pl.BlockSpec((1, D), lambda b: (0, 0)),
                  pl.BlockSpec(w_kv_t.shape, lambda b: (0, 0))],
        out_specs=pl.BlockSpec((1, 2 * MEM_WIDTH, n_mem), lambda b: (b, 0, 0)),
        name="memkv",
    )(mem, g_mem.reshape(1, -1), w_kv_t)


def _memory_attention(p_scr, kvm_ref, om_ref, lanes):
    heads = range(N_MEM_HEADS)
    s = []
    for hm in heads:
        r0 = 3 * SELF_WIDTH + hm * HEAD_DIM
        qm = (p_scr[r0:r0 + HEAD_DIM, :] * QK_SCALE).astype(BF16)
        km = kvm_ref[0, hm * HEAD_DIM:(hm + 1) * HEAD_DIM, :]
        s.append(_dot_t(km, qm))
    p = [jnp.exp(s[hm] - jnp.max(s[hm], axis=0, keepdims=True)) for hm in heads]
    for hm in heads:
        vm = kvm_ref[0, MEM_WIDTH + hm * HEAD_DIM:MEM_WIDTH + (hm + 1) * HEAD_DIM, :]
        l = jnp.sum(p[hm], axis=0, keepdims=True)
        o = jnp.dot(vm, p[hm].astype(BF16), preferred_element_type=F32)
        om_ref[0, hm * HEAD_DIM:(hm + 1) * HEAD_DIM, lanes] = (o * (1.0 / l)).astype(BF16)


def _rotary_and_gate(p_scr, own, pos, freq_ref, sel_ref, km_scr, lanes):
    ang = freq_ref[...] * pos.astype(F32)
    n = jnp.floor(ang * _INV_TWO_PI + 0.5)
    red = ((ang - n * _C1) - n * _C2) - n * _C3
    cos, sin = jnp.cos(red), jnp.sin(red)
    for base in range(0, 2 * SELF_WIDTH, HEAD_DIM):
        x1 = p_scr[base:base + ROT_HALF, :]
        x2 = p_scr[base + ROT_HALF:base + 2 * ROT_HALF, :]
        p_scr[base:base + ROT_HALF, :] = x1 * cos - x2 * sin
        p_scr[base + ROT_HALF:base + 2 * ROT_HALF, :] = x2 * cos + x1 * sin

    tile = p_scr.shape[1]
    n_blk = km_scr.shape[1]
    blk_lane = lax.broadcasted_iota(jnp.int32, (HEAD_DIM, n_blk), 1)
    blk = lax.broadcasted_iota(jnp.int32, (n_blk, tile), 0)
    gates = [_dot_t(km_scr[hd * HEAD_DIM:(hd + 1) * HEAD_DIM, :], p_scr[hd * HEAD_DIM:(hd + 1) * HEAD_DIM, :] * QK_SCALE,
                    precision=lax.Precision.HIGHEST) for hd in range(N_SELF_HEADS)]
    for hd in range(N_SELF_HEADS):
        r0 = hd * HEAD_DIM
        gate = jnp.where(blk < own, gates[hd], -jnp.inf)
        picked = jnp.zeros(gate.shape, jnp.bool_)
        for k in range(MOBA_TOPK):
            top = jnp.max(gate, axis=0, keepdims=True)
            first = jnp.min(jnp.where(gate == top, blk, n_blk), axis=0, keepdims=True)
            hit = blk == first
            picked = jnp.logical_or(picked, jnp.logical_and(hit, k < own))
            gate = jnp.where(hit, -jnp.inf, gate)
        sel_ref[0, hd, :, lanes] = jnp.where(picked, 0.0, -jnp.inf).astype(BF16)
        kmean = jnp.mean(p_scr[SELF_WIDTH + r0:SELF_WIDTH + r0 + HEAD_DIM, :], axis=1, keepdims=True)
        km_scr[r0:r0 + HEAD_DIM, :] = jnp.where(blk_lane == own, kmean, km_scr[r0:r0 + HEAD_DIM, :])


def _proj_body(*refs, rotary_gate):
    it = iter(refs)
    h_ref, g_ref, w_ref, kvm_ref = next(it), next(it), next(it), next(it)
    if rotary_gate:
        pos_ref, freq_ref = next(it), next(it)
    q_ref, k_ref, v_ref, om_ref = next(it), next(it), next(it), next(it)
    if rotary_gate:
        sel_ref = next(it)
    p_scrs = [next(it) for _ in range(PROJ_SUBTILES)]
    if rotary_gate:
        km_scr = next(it)
    tile = MOBA_BLOCK

    for sub, p_scr in enumerate(p_scrs):
        u = (_rms_scale(h_ref[0, sub * tile:(sub + 1) * tile, :]) * g_ref[...]).astype(BF16)
        p_scr[...] = _dot_nt(w_ref[...], u)

    if rotary_gate:
        @pl.when(pl.program_id(1) == 0)
        def _():
            km_scr[...] = jnp.zeros_like(km_scr)

    for sub, p_scr in enumerate(p_scrs):
        lanes = slice(sub * tile, (sub + 1) * tile)
        if rotary_gate:
            own = pl.program_id(1) * PROJ_SUBTILES + sub
            _rotary_and_gate(p_scr, own, pos_ref[0, :, lanes], freq_ref, sel_ref, km_scr, lanes)
        q_scale = QK_SCALE * LOG2_E if rotary_gate else -QK_SCALE * LOG2_E
        q_ref[0, :, lanes] = (p_scr[0:SELF_WIDTH, :] * q_scale).astype(BF16)
        for hp in range(N_SELF_HEADS // 2):
            r0 = SELF_WIDTH + hp * 2 * HEAD_DIM
            k_ref[0, hp, lanes, :] = p_scr[r0:r0 + 2 * HEAD_DIM, :].T.astype(BF16)
        v_ref[0, :, lanes] = p_scr[2 * SELF_WIDTH:3 * SELF_WIDTH, :].astype(BF16)
        _memory_attention(p_scr, kvm_ref, om_ref, lanes)


def _proj(h, g, w_in_t, kvm, rope=None):
    B, S, D = h.shape
    pw = w_in_t.shape[0]
    tm = MOBA_BLOCK * PROJ_SUBTILES
    assert S % tm == 0
    n_blk = S // MOBA_BLOCK
    grid = (B, S // tm)
    const2 = lambda b, i: (0, 0)
    in_specs = [pl.BlockSpec((1, tm, D), lambda b, i: (b, i, 0)),
                pl.BlockSpec((1, D), const2),
                pl.BlockSpec(w_in_t.shape, const2, pipeline_mode=pl.Buffered(1)),
                pl.BlockSpec((1,) + kvm.shape[1:], lambda b, i: (b, 0, 0))]
    args = [h, g.reshape(1, D), w_in_t, kvm]
    slab = lambda rows: pl.BlockSpec((1, rows, tm), lambda b, i: (b, 0, i))
    n_pairs = N_SELF_HEADS // 2
    out_shape = [jax.ShapeDtypeStruct((B, SELF_WIDTH, S), BF16),
                 jax.ShapeDtypeStruct((B, n_pairs, S, 2 * HEAD_DIM), BF16),
                 jax.ShapeDtypeStruct((B, SELF_WIDTH, S), BF16),
                 jax.ShapeDtypeStruct((B, MEM_WIDTH, S), BF16)]
    out_specs = [slab(SELF_WIDTH),
                 pl.BlockSpec((1, n_pairs, tm, 2 * HEAD_DIM), lambda b, i: (b, 0, i, 0)),
                 slab(SELF_WIDTH), slab(MEM_WIDTH)]
    scratch = [pltpu.VMEM((pw, MOBA_BLOCK), F32)] * PROJ_SUBTILES
    if rope is not None:
        positions, inv_freq = rope
        in_specs += [pl.BlockSpec((1, 1, tm), lambda b, i: (b, 0, i)),
                     pl.BlockSpec((ROT_HALF, 1), const2)]
        args += [positions.reshape(B, 1, S), inv_freq.reshape(ROT_HALF, 1)]
        out_shape.append(jax.ShapeDtypeStruct((B, N_SELF_HEADS, n_blk, S), BF16))
        out_specs.append(pl.BlockSpec((1, N_SELF_HEADS, n_blk, tm), lambda b, i: (b, 0, 0, i)))
        scratch.append(pltpu.VMEM((SELF_WIDTH, n_blk), F32))
    return pl.pallas_call(
        functools.partial(_proj_body, rotary_gate=rope is not None),
        out_shape=out_shape,
        grid=grid,
        in_specs=in_specs,
        out_specs=out_specs,
        scratch_shapes=scratch,
        compiler_params=pltpu.CompilerParams(dimension_semantics=("arbitrary", "arbitrary"),
                                             vmem_limit_bytes=VMEM_LIMIT),
        name="proj_moba" if rope is not None else "proj_sb",
    )(*args)


def _pair_queries(q):
    d, t = HEAD_DIM, q.shape[1]
    zero = jnp.zeros((d, t), q.dtype)
    return jnp.concatenate([jnp.concatenate([q[:d], zero], axis=1), jnp.concatenate([zero, q[d:]], axis=1)], axis=0)


def _moba_body(q_ref, k_ref, v_ref, sel_ref, o_ref, bias_scr, *s_scr):
    t, grp, d, pairs = ATT_TILE, MOBA_GROUP, HEAD_DIM, range(MOBA_PAIRS)
    i = pl.program_id(2)
    qq = [_pair_queries(q_ref[0, 2 * d * pr:2 * d * (pr + 1)]) for pr in pairs]
    bias_scr[...] = jnp.concatenate([sel_ref[0, hd] for hd in range(2 * MOBA_PAIRS)], axis=1).astype(F32)
    ones = jnp.ones((ONES_ROWS, t), BF16)
    last_group = bias_scr.shape[0] // grp - 1

    def bias(j, pr):
        return bias_scr[pl.ds(j, 1), 2 * t * pr:2 * t * (pr + 1)]

    def weighted_values(pr, off, p):
        out = []
        for hd in range(2):
            r0 = (2 * pr + hd) * d
            v1 = jnp.concatenate([v_ref[0, r0:r0 + d, pl.ds(off, t)], ones], axis=0)
            out.append(jnp.dot(v1, p[:, hd * t:(hd + 1) * t], preferred_element_type=F32))
        return jnp.concatenate(out, axis=1)

    def score_block(jg, g, pr):
        off = pl.multiple_of((jg * grp + g) * t, t)
        s = jnp.dot(k_ref[0, pr, pl.ds(off, t), :], qq[pr], preferred_element_type=F32)
        s_scr[g * MOBA_PAIRS + pr][...] = s
        return jnp.max(s, axis=0, keepdims=True) + bias(jg * grp + g, pr)

    def score_group(jg):
        cm = [None] * MOBA_PAIRS
        for g in range(grp):
            for pr in pairs:
                c = score_block(jg, g, pr)
                cm[pr] = c if cm[pr] is None else jnp.maximum(cm[pr], c)
        return tuple(cm)

    off_i = pl.multiple_of(i * t, t)
    key = lax.broadcasted_iota(jnp.int32, (t, t), 0)
    qry = lax.broadcasted_iota(jnp.int32, (t, t), 1)
    causal = jnp.concatenate([key <= qry] * 2, axis=1)
    m, acc = [], []
    for pr in pairs:
        s = jnp.dot(k_ref[0, pr, pl.ds(off_i, t), :], qq[pr], preferred_element_type=F32)
        s = jnp.where(causal, s, -jnp.inf)
        m.append(jnp.max(s, axis=0, keepdims=True))
        acc.append(weighted_values(pr, off_i, jnp.exp2(s - m[pr]).astype(BF16)))

    def step(n, carry):
        m, acc, cm = carry
        m_new = [jnp.maximum(m[pr], cm[pr]) for pr in pairs]
        nxt = jnp.minimum(n + 1, last_group)
        pv, cm_next = [None] * MOBA_PAIRS, [None] * MOBA_PAIRS
        for g in range(grp):
            off = pl.multiple_of((n * grp + g) * t, t)
            for pr in pairs:
                s = s_scr[g * MOBA_PAIRS + pr][...]
                p = jnp.exp2(s + (bias(n * grp + g, pr) - m_new[pr])).astype(BF16)
                w = weighted_values(pr, off, p)
                pv[pr] = w if pv[pr] is None else pv[pr] + w
                c = score_block(nxt, g, pr)
                cm_next[pr] = c if cm_next[pr] is None else jnp.maximum(cm_next[pr], c)
        acc = [jnp.exp2(m[pr] - m_new[pr]) * acc[pr] + pv[pr] for pr in pairs]
        return tuple(m_new), tuple(acc), tuple(cm_next)

    carry = (tuple(m), tuple(acc), score_group(0))
    _, acc, _ = lax.fori_loop(0, lax.div(i + (grp - 1), grp), step, carry)
    for pr in pairs:
        out = acc[pr][:d] * (1.0 / acc[pr][d:d + 1])
        o_ref[0, 2 * d * pr:2 * d * (pr + 1)] = jnp.concatenate([out[:, :t], out[:, t:]], axis=0).astype(BF16)


def _moba_attention(q, k, v, sel):
    B, W, S = q.shape
    t = ATT_TILE
    n_blk = S // t
    assert n_blk % MOBA_GROUP == 0 and N_SELF_HEADS % (2 * MOBA_PAIRS) == 0
    rows = 2 * HEAD_DIM * MOBA_PAIRS
    q_tile = pl.BlockSpec((1, rows, t), lambda b, hq, i: (b, hq, i))
    resident = dict(pipeline_mode=pl.Buffered(1))
    return pl.pallas_call(
        _moba_body,
        out_shape=jax.ShapeDtypeStruct((B, W, S), BF16),
        grid=(B, N_SELF_HEADS // (2 * MOBA_PAIRS), n_blk),
        in_specs=[q_tile,
                  pl.BlockSpec((1, MOBA_PAIRS, S, 2 * HEAD_DIM), lambda b, hq, i: (b, hq, 0, 0), **resident),
                  pl.BlockSpec((1, rows, S), lambda b, hq, i: (b, hq, 0), **resident),
                  pl.BlockSpec((1, 2 * MOBA_PAIRS, n_blk, t), lambda b, hq, i: (b, hq, 0, i))],
        out_specs=q_tile,
        scratch_shapes=([pltpu.VMEM((n_blk, 2 * t * MOBA_PAIRS), F32)]
                        + [pltpu.VMEM((t, 2 * t), F32)] * (MOBA_GROUP * MOBA_PAIRS)),
        compiler_params=pltpu.CompilerParams(dimension_semantics=("arbitrary",) * 3, vmem_limit_bytes=VMEM_LIMIT),
        name="moba_attn",
    )(q, k, v, sel)


def _sb_body(q_ref, k_ref, v_ref, o_ref):
    t, d, pairs = ATT_TILE, HEAD_DIM, range(SB_PAIRS)
    i = pl.program_id(2)
    qq = [_pair_queries(q_ref[0, 2 * d * pr:2 * d * (pr + 1)]) for pr in pairs]
    key = lax.broadcasted_iota(jnp.int32, (t, t), 0)
    qry = lax.broadcasted_iota(jnp.int32, (t, t), 1)
    tri = (qry >= key).astype(BF16)
    tri2 = jnp.concatenate([tri, tri], axis=1)
    causal = jnp.concatenate([key < qry] * 2, axis=1)

    def scores(pr, off):
        return jnp.dot(k_ref[0, pr, pl.ds(off, t), :], qq[pr], preferred_element_type=F32)

    def suffix_sums(zn, masked):
        log_1m = jnp.minimum(zn, 0.0) - jnp.log(1.0 + jnp.exp2(-jnp.abs(zn))) * LOG2_E
        if masked:
            log_1m = jnp.where(causal, log_1m, 0.0)
        hi = log_1m.astype(BF16)
        lo = (log_1m - hi.astype(F32)).astype(BF16)
        return jnp.dot(tri2, jnp.concatenate([hi, lo], axis=0), preferred_element_type=F32)

    def weighted_values(pr, off, a):
        out = []
        for hd in range(2):
            r0 = (2 * pr + hd) * d
            out.append(jnp.dot(v_ref[0, r0:r0 + d, pl.ds(off, t)], a[:, hd * t:(hd + 1) * t],
                               preferred_element_type=F32))
        return jnp.concatenate(out, axis=1)

    off0 = pl.multiple_of(i * t, t)
    off1 = pl.multiple_of(jnp.maximum(i - 1, 0) * t, t)
    has_prev = i >= 1
    z0 = [scores(pr, off0) for pr in pairs]
    z1 = [scores(pr, off1) for pr in pairs]
    incl0 = [suffix_sums(z0[pr], True) for pr in pairs]
    incl1 = [suffix_sums(z1[pr], False) for pr in pairs]
    carry, acc = [], []
    for pr in pairs:
        a0 = jnp.where(causal, jnp.exp2(incl0[pr] - z0[pr]), 0.0)
        carry0 = incl0[pr][0:1, :]
        a1 = jnp.exp2((incl1[pr] + (carry0 + jnp.where(has_prev, 0.0, NEG_BIAS))) - z1[pr])
        acc.append(weighted_values(pr, off0, a0.astype(BF16)) + weighted_values(pr, off1, a1.astype(BF16)))
        carry.append(carry0 + jnp.where(has_prev, incl1[pr][0:1, :], 0.0))

    def liveness(carry):
        live = jnp.max(carry[0])
        for pr in pairs[1:]:
            live = jnp.maximum(live, jnp.max(carry[pr]))
        return live

    def alive(state):
        j, _, _, live = state
        return jnp.logical_and(j >= 0, live > SB_DEAD_LOG2)

    def step(state):
        j, carry, acc, _ = state
        off = pl.multiple_of(j * t, t)
        carry, acc = list(carry), list(acc)
        z = [scores(pr, off) for pr in pairs]
        for pr in pairs:
            incl = suffix_sums(z[pr], False)
            acc[pr] = acc[pr] + weighted_values(pr, off, jnp.exp2((incl + carry[pr]) - z[pr]).astype(BF16))
            carry[pr] = carry[pr] + incl[0:1, :]
        return j - 1, tuple(carry), tuple(acc), liveness(carry)

    _, _, acc, _ = lax.while_loop(alive, step, (i - 2, tuple(carry), tuple(acc), liveness(carry)))
    for pr in pairs:
        o_ref[0, 2 * d * pr:2 * d * (pr + 1)] = jnp.concatenate([acc[pr][:, :t], acc[pr][:, t:]], axis=0).astype(BF16)


def _sb_attention(q, k, v):
    B, W, S = q.shape
    t = ATT_TILE
    assert N_SELF_HEADS % (2 * SB_PAIRS) == 0
    rows = 2 * HEAD_DIM * SB_PAIRS
    q_tile = pl.BlockSpec((1, rows, t), lambda b, hq, i: (b, hq, i))
    resident = dict(pipeline_mode=pl.Buffered(1))
    return pl.pallas_call(
        _sb_body,
        out_shape=jax.ShapeDtypeStruct((B, W, S), BF16),
        grid=(B, N_SELF_HEADS // (2 * SB_PAIRS), S // t),
        in_specs=[q_tile,
                  pl.BlockSpec((1, SB_PAIRS, S, 2 * HEAD_DIM), lambda b, hq, i: (b, hq, 0, 0), **resident),
                  pl.BlockSpec((1, rows, S), lambda b, hq, i: (b, hq, 0), **resident)],
        out_specs=q_tile,
        compiler_params=pltpu.CompilerParams(dimension_semantics=("arbitrary",) * 3, vmem_limit_bytes=VMEM_LIMIT),
        name="sb_attn",
    )(q, k, v)


def kernel(x, mem, positions, g_ffn_pre, w_pre_gate, w_pre_up, w_pre_down, g_mix, w_in, g_mem, w_mem_kv, w_out,
           g_ffn_post, w_post_gate, w_post_up, w_post_down, g_final):
    depth = w_in.shape[0]
    inv_freq = ROPE_THETA ** (-jnp.arange(0, 2 * ROT_HALF, 2, dtype=F32) / (2 * ROT_HALF))
    h = x
    for layer in range(depth):
        h = _ffn(h, g_ffn_pre[layer], w_pre_gate[layer].astype(BF16), w_pre_up[layer].astype(BF16),
                 w_pre_down[layer].astype(BF16))
        kvm = _memkv(mem, g_mem[layer], w_mem_kv[layer].T.astype(BF16))
        w_in_t = w_in[layer].T.astype(BF16)
        if layer % 2 == 0:
            q, k, v, o_mem, sel = _proj(h, g_mix[layer], w_in_t, kvm, rope=(positions, inv_freq))
            o_self = _moba_attention(q, k, v, sel)
        else:
            q, k, v, o_mem = _proj(h, g_mix[layer], w_in_t, kvm)
            o_self = _sb_attention(q, k, v)
        w_o = w_out[layer].astype(BF16)
        h = _ffn(h, g_ffn_post[layer], w_post_gate[layer].astype(BF16), w_post_up[layer].astype(BF16),
                 w_post_down[layer].astype(BF16),
                 attn=(o_self, o_mem, w_o[:SELF_WIDTH], w_o[SELF_WIDTH:]),
                 final_g=g_final if layer == depth - 1 else None)
    return h
```

```python
import functools
import math

import numpy as np
import jax
import jax.numpy as jnp
from jax import lax
from jax.experimental import pallas as pl
from jax.experimental.pallas import tpu as pltpu

F32 = jnp.float32
BF16 = jnp.bfloat16

HEAD_DIM = 64
N_SELF_HEADS = 12
N_MEM_HEADS = 4
SELF_WIDTH = N_SELF_HEADS * HEAD_DIM
MEM_WIDTH = N_MEM_HEADS * HEAD_DIM
ROT_HALF = 8
ROPE_THETA = 500000.0
MOBA_BLOCK = 256
MOBA_TOPK = 3
RMS_EPS = 1e-6
NEG_BIAS = -1e30
QK_SCALE = 1.0 / math.sqrt(HEAD_DIM)
LOG2_E = math.log2(math.e)
ONES_ROWS = 16

FFN_ROWS = 512
FFN_CHUNK = 256
PROJ_SUBTILES = 2
ATT_TILE = 256
MOBA_GROUP = 4
MOBA_PAIRS = 3
SB_PAIRS = 3
SB_DEAD_LOG2 = -160.0
VMEM_LIMIT = 56 * 1024 * 1024

_TWO_PI = 2.0 * math.pi
_C1 = 6.28125
_C2 = round((_TWO_PI - _C1) * 2 ** 20) / 2 ** 20
_C3 = float(np.float32(_TWO_PI - _C1 - _C2))
_INV_TWO_PI = float(np.float32(1.0 / _TWO_PI))


def _rms_scale(x):
    return x * lax.rsqrt(jnp.mean(x * x, axis=-1, keepdims=True) + RMS_EPS)


def _dot_t(a, b, **kw):
    return lax.dot_general(a, b, (((0,), (0,)), ((), ())), preferred_element_type=F32, **kw)


def _dot_nt(a, b):
    return lax.dot_general(a, b, (((1,), (1,)), ((), ())), preferred_element_type=F32)


def _ffn_body(*refs, n_chunks, with_attn, with_final):
    it = iter(refs)
    x_ref = next(it)
    if with_attn:
        os_ref, om_ref, wos_ref, wom_ref = next(it), next(it), next(it), next(it)
    g_ref, wg_ref, wu_ref, wd_ref = next(it), next(it), next(it), next(it)
    gf_ref = next(it) if with_final else None
    o_ref, x_scr, u_scr, acc_scr = next(it), next(it), next(it), next(it)

    x = x_ref[...]
    if with_attn:
        x = x + _dot_t(os_ref[0], wos_ref[...]) + _dot_t(om_ref[0], wom_ref[...])
    x_scr[...] = x
    u_scr[...] = (_rms_scale(x) * g_ref[...]).astype(BF16)
    acc_scr[...] = jnp.zeros_like(acc_scr)

    for c in range(n_chunks):
        cols = slice(c * FFN_CHUNK, (c + 1) * FFN_CHUNK)
        u = u_scr[...]
        gate = jnp.dot(u, wg_ref[:, cols], preferred_element_type=F32)
        up = jnp.dot(u, wu_ref[:, cols], preferred_element_type=F32)
        act = (gate * jax.nn.sigmoid(gate)) * up
        acc_scr[...] += jnp.dot(act.astype(BF16), wd_ref[cols, :], preferred_element_type=F32)
    y = x_scr[...] + 0.5 * acc_scr[...]
    if with_final:
        y = _rms_scale(y) * gf_ref[...]
    o_ref[...] = y


def _ffn(h, g, wg, wu, wd, attn=None, final_g=None):
    B, S, D = h.shape
    assert wg.shape[1] % FFN_CHUNK == 0
    n_chunks = wg.shape[1] // FFN_CHUNK
    tm = FFN_ROWS
    assert S % tm == 0
    grid = (B, S // tm)
    const2 = lambda b, i: (0, 0)
    resident = dict(pipeline_mode=pl.Buffered(1))
    in_specs = [pl.BlockSpec((None, tm, D), lambda b, i: (b, i, 0))]
    args = [h]
    if attn is not None:
        o_self, o_mem, w_os, w_om = attn
        in_specs += [
            pl.BlockSpec((1, SELF_WIDTH, tm), lambda b, i: (b, 0, i)),
            pl.BlockSpec((1, MEM_WIDTH, tm), lambda b, i: (b, 0, i)),
            pl.BlockSpec(w_os.shape, const2, **resident),
            pl.BlockSpec(w_om.shape, const2, **resident),
        ]
        args += [o_self, o_mem, w_os, w_om]
    in_specs += [
        pl.BlockSpec((1, D), const2, **resident),
        pl.BlockSpec(wg.shape, const2, **resident),
        pl.BlockSpec(wu.shape, const2, **resident),
        pl.BlockSpec(wd.shape, const2, **resident),
    ]
    args += [g.reshape(1, D), wg, wu, wd]
    if final_g is not None:
        in_specs.append(pl.BlockSpec((1, D), const2, **resident))
        args.append(final_g.reshape(1, D))
    body = functools.partial(_ffn_body, n_chunks=n_chunks, with_attn=attn is not None,
                             with_final=final_g is not None)
    return pl.pallas_call(
        body,
        out_shape=jax.ShapeDtypeStruct((B, S, D), F32),
        grid=grid,
        in_specs=in_specs,
        out_specs=pl.BlockSpec((None, tm, D), lambda b, i: (b, i, 0)),
        scratch_shapes=[pltpu.VMEM((tm, D), F32), pltpu.VMEM((tm, D), BF16), pltpu.VMEM((tm, D), F32)],
        compiler_params=pltpu.CompilerParams(dimension_semantics=("arbitrary", "arbitrary"),
                                             vmem_limit_bytes=VMEM_LIMIT),
        name="ffn_attn" if attn is not None else "ffn",
    )(*args)


def _memkv_body(mem_ref, g_ref, w_ref, o_ref):
    u = (_rms_scale(mem_ref[0]) * g_ref[...]).astype(BF16)
    o_ref[0] = _dot_nt(w_ref[...], u).astype(BF16)


def _memkv(mem, g_mem, w_kv_t):
    B, n_mem, D = mem.shape
    return pl.pallas_call(
        _memkv_body,
        out_shape=jax.ShapeDtypeStruct((B, 2 * MEM_WIDTH, n_mem), BF16),
        grid=(B,),
        in_specs=[pl.BlockSpec((1, n_mem, D), lambda b: (b, 0, 0)),
                  pl.BlockSpec((1, D), lambda b: (0, 0)),
                  pl.BlockSpec(w_kv_t.shape, lambda b: (0, 0))],
        out_specs=pl.BlockSpec((1, 2 * MEM_WIDTH, n_mem), lambda b: (b, 0, 0)),
        name="memkv",
    )(mem, g_mem.reshape(1, -1), w_kv_t)


def _memory_attention(p_scr, kvm_ref, om_ref, lanes):
    heads = range(N_MEM_HEADS)
    s = []
    for hm in heads:
        r0 = 3 * SELF_WIDTH + hm * HEAD_DIM
        qm = (p_scr[r0:r0 + HEAD_DIM, :] * QK_SCALE).astype(BF16)
        km = kvm_ref[0, hm * HEAD_DIM:(hm + 1) * HEAD_DIM, :]
        s.append(_dot_t(km, qm))
    p = [jnp.exp(s[hm] - jnp.max(s[hm], axis=0, keepdims=True)) for hm in heads]
    for hm in heads:
        vm = kvm_ref[0, MEM_WIDTH + hm * HEAD_DIM:MEM_WIDTH + (hm + 1) * HEAD_DIM, :]
        l = jnp.sum(p[hm], axis=0, keepdims=True)
        o = jnp.dot(vm, p[hm].astype(BF16), preferred_element_type=F32)
        om_ref[0, hm * HEAD_DIM:(hm + 1) * HEAD_DIM, lanes] = (o * (1.0 / l)).astype(BF16)


def _rotary_and_gate(p_scr, own, pos, freq_ref, sel_ref, km_scr, lanes):
    ang = freq_ref[...] * pos.astype(F32)
    n = jnp.floor(ang * _INV_TWO_PI + 0.5)
    red = ((ang - n * _C1) - n * _C2) - n * _C3
    cos, sin = jnp.cos(red), jnp.sin(red)
    for base in range(0, 2 * SELF_WIDTH, HEAD_DIM):
        x1 = p_scr[base:base + ROT_HALF, :]
        x2 = p_scr[base + ROT_HALF:base + 2 * ROT_HALF, :]
        p_scr[base:base + ROT_HALF, :] = x1 * cos - x2 * sin
        p_scr[base + ROT_HALF:base + 2 * ROT_HALF, :] = x2 * cos + x1 * sin

    tile = p_scr.shape[1]
    n_blk = km_scr.shape[1]
    blk_lane = lax.broadcasted_iota(jnp.int32, (HEAD_DIM, n_blk), 1)
    blk = lax.broadcasted_iota(jnp.int32, (n_blk, tile), 0)
    gates = [_dot_t(km_scr[hd * HEAD_DIM:(hd + 1) * HEAD_DIM, :], p_scr[hd * HEAD_DIM:(hd + 1) * HEAD_DIM, :] * QK_SCALE,
                    precision=lax.Precision.HIGHEST) for hd in range(N_SELF_HEADS)]
    for hd in range(N_SELF_HEADS):
        r0 = hd * HEAD_DIM
        gate = jnp.where(blk < own, gates[hd], -jnp.inf)
        picked = jnp.zeros(gate.shape, jnp.bool_)
        for k in range(MOBA_TOPK):
            top = jnp.max(gate, axis=0, keepdims=True)
            first = jnp.min(jnp.where(gate == top, blk, n_blk), axis=0, keepdims=True)
            hit = blk == first
            picked = jnp.logical_or(picked, jnp.logical_and(hit, k < own))
            gate = jnp.where(hit, -jnp.inf, gate)
        sel_ref[0, hd, :, lanes] = jnp.where(picked, 0.0, -jnp.inf).astype(BF16)
        kmean = jnp.mean(p_scr[SELF_WIDTH + r0:SELF_WIDTH + r0 + HEAD_DIM, :], axis=1, keepdims=True)
        km_scr[r0:r0 + HEAD_DIM, :] = jnp.where(blk_lane == own, kmean, km_scr[r0:r0 + HEAD_DIM, :])


def _proj_body(*refs, rotary_gate):
    it = iter(refs)
    h_ref, g_ref, w_ref, kvm_ref = next(it), next(it), next(it), next(it)
    if rotary_gate:
        pos_ref, freq_ref = next(it), next(it)
    q_ref, k_ref, v_ref, om_ref = next(it), next(it), next(it), next(it)
    if rotary_gate:
        sel_ref = next(it)
    p_scrs = [next(it) for _ in range(PROJ_SUBTILES)]
    if rotary_gate:
        km_scr = next(it)
    tile = MOBA_BLOCK

    for sub, p_scr in enumerate(p_scrs):
        u = (_rms_scale(h_ref[0, sub * tile:(sub + 1) * tile, :]) * g_ref[...]).astype(BF16)
        p_scr[...] = _dot_nt(w_ref[...], u)

    if rotary_gate:
        @pl.when(pl.program_id(1) == 0)
        def _():
            km_scr[...] = jnp.zeros_like(km_scr)

    for sub, p_scr in enumerate(p_scrs):
        lanes = slice(sub * tile, (sub + 1) * tile)
        if rotary_gate:
            own = pl.program_id(1) * PROJ_SUBTILES + sub
            _rotary_and_gate(p_scr, own, pos_ref[0, :, lanes], freq_ref, sel_ref, km_scr, lanes)
        q_scale = QK_SCALE * LOG2_E if rotary_gate else -QK_SCALE * LOG2_E
        q_ref[0, :, lanes] = (p_scr[0:SELF_WIDTH, :] * q_scale).astype(BF16)
        for hp in range(N_SELF_HEADS // 2):
            r0 = SELF_WIDTH + hp * 2 * HEAD_DIM
            k_ref[0, hp, lanes, :] = p_scr[r0:r0 + 2 * HEAD_DIM, :].T.astype(BF16)
        v_ref[0, :, lanes] = p_scr[2 * SELF_WIDTH:3 * SELF_WIDTH, :].astype(BF16)
        _memory_attention(p_scr, kvm_ref, om_ref, lanes)


def _proj(h, g, w_in_t, kvm, rope=None):
    B, S, D = h.shape
    pw = w_in_t.shape[0]
    tm = MOBA_BLOCK * PROJ_SUBTILES
    assert S % tm == 0
    n_blk = S // MOBA_BLOCK
    grid = (B, S // tm)
    const2 = lambda b, i: (0, 0)
    in_specs = [pl.BlockSpec((1, tm, D), lambda b, i: (b, i, 0)),
                pl.BlockSpec((1, D), const2),
                pl.BlockSpec(w_in_t.shape, const2, pipeline_mode=pl.Buffered(1)),
                pl.BlockSpec((1,) + kvm.shape[1:], lambda b, i: (b, 0, 0))]
    args = [h, g.reshape(1, D), w_in_t, kvm]
    slab = lambda rows: pl.BlockSpec((1, rows, tm), lambda b, i: (b, 0, i))
    n_pairs = N_SELF_HEADS // 2
    out_shape = [jax.ShapeDtypeStruct((B, SELF_WIDTH, S), BF16),
                 jax.ShapeDtypeStruct((B, n_pairs, S, 2 * HEAD_DIM), BF16),
                 jax.ShapeDtypeStruct((B, SELF_WIDTH, S), BF16),
                 jax.ShapeDtypeStruct((B, MEM_WIDTH, S), BF16)]
    out_specs = [slab(SELF_WIDTH),
                 pl.BlockSpec((1, n_pairs, tm, 2 * HEAD_DIM), lambda b, i: (b, 0, i, 0)),
                 slab(SELF_WIDTH), slab(MEM_WIDTH)]
    scratch = [pltpu.VMEM((pw, MOBA_BLOCK), F32)] * PROJ_SUBTILES
    if rope is not None:
        positions, inv_freq = rope
        in_specs += [pl.BlockSpec((1, 1, tm), lambda b, i: (b, 0, i)),
                     pl.BlockSpec((ROT_HALF, 1), const2)]
        args += [positions.reshape(B, 1, S), inv_freq.reshape(ROT_HALF, 1)]
        out_shape.append(jax.ShapeDtypeStruct((B, N_SELF_HEADS, n_blk, S), BF16))
        out_specs.append(pl.BlockSpec((1, N_SELF_HEADS, n_blk, tm), lambda b, i: (b, 0, 0, i)))
        scratch.append(pltpu.VMEM((SELF_WIDTH, n_blk), F32))
    return pl.pallas_call(
        functools.partial(_proj_body, rotary_gate=rope is not None),
        out_shape=out_shape,
        grid=grid,
        in_specs=in_specs,
        out_specs=out_specs,
        scratch_shapes=scratch,
        compiler_params=pltpu.CompilerParams(dimension_semantics=("arbitrary", "arbitrary"),
                                             vmem_limit_bytes=VMEM_LIMIT),
        name="proj_moba" if rope is not None else "proj_sb",
    )(*args)


def _pair_queries(q):
    d, t = HEAD_DIM, q.shape[1]
    zero = jnp.zeros((d, t), q.dtype)
    return jnp.concatenate([jnp.concatenate([q[:d], zero], axis=1), jnp.concatenate([zero, q[d:]], axis=1)], axis=0)


def _moba_body(q_ref, k_ref, v_ref, sel_ref, o_ref, bias_scr, *s_scr):
    t, grp, d, pairs = ATT_TILE, MOBA_GROUP, HEAD_DIM, range(MOBA_PAIRS)
    i = pl.program_id(2)
    qq = [_pair_queries(q_ref[0, 2 * d * pr:2 * d * (pr + 1)]) for pr in pairs]
    bias_scr[...] = jnp.concatenate([sel_ref[0, hd] for hd in range(2 * MOBA_PAIRS)], axis=1).astype(F32)
    ones = jnp.ones((ONES_ROWS, t), BF16)
    last_group = bias_scr.shape[0] // grp - 1

    def bias(j, pr):
        return bias_scr[pl.ds(j, 1), 2 * t * pr:2 * t * (pr + 1)]

    def weighted_values(pr, off, p):
        out = []
        for hd in range(2):
            r0 = (2 * pr + hd) * d
            v1 = jnp.concatenate([v_ref[0, r0:r0 + d, pl.ds(off, t)], ones], axis=0)
            out.append(jnp.dot(v1, p[:, hd * t:(hd + 1) * t], preferred_element_type=F32))
        return jnp.concatenate(out, axis=1)

    def score_block(jg, g, pr):
        off = pl.multiple_of((jg * grp + g) * t, t)
        s = jnp.dot(k_ref[0, pr, pl.ds(off, t), :], qq[pr], preferred_element_type=F32)
        s_scr[g * MOBA_PAIRS + pr][...] = s
        return jnp.max(s, axis=0, keepdims=True) + bias(jg * grp + g, pr)

    def score_group(jg):
        cm = [None] * MOBA_PAIRS
        for g in range(grp):
            for pr in pairs:
                c = score_block(jg, g, pr)
                cm[pr] = c if cm[pr] is None else jnp.maximum(cm[pr], c)
        return tuple(cm)

    off_i = pl.multiple_of(i * t, t)
    key = lax.broadcasted_iota(jnp.int32, (t, t), 0)
    qry = lax.broadcasted_iota(jnp.int32, (t, t), 1)
    causal = jnp.concatenate([key <= qry] * 2, axis=1)
    s_own = [jnp.dot(k_ref[0, pr, pl.ds(off_i, t), :], qq[pr], preferred_element_type=F32) for pr in pairs]
    cm0 = score_group(0)
    s_own = [jnp.where(causal, s_own[pr], -jnp.inf) for pr in pairs]
    m = [jnp.max(s_own[pr], axis=0, keepdims=True) for pr in pairs]
    acc = [weighted_values(pr, off_i, jnp.exp2(s_own[pr] - m[pr]).astype(BF16)) for pr in pairs]

    def step(n, carry):
        m, acc, cm = carry
        m_new = [jnp.maximum(m[pr], cm[pr]) for pr in pairs]
        nxt = jnp.minimum(n + 1, last_group)
        pv, cm_next = [None] * MOBA_PAIRS, [None] * MOBA_PAIRS
        for g in range(grp):
            off = pl.multiple_of((n * grp + g) * t, t)
            for pr in pairs:
                s = s_scr[g * MOBA_PAIRS + pr][...]
                c = score_block(nxt, g, pr)
                cm_next[pr] = c if cm_next[pr] is None else jnp.maximum(cm_next[pr], c)
                p = jnp.exp2(s + (bias(n * grp + g, pr) - m_new[pr])).astype(BF16)
                w = weighted_values(pr, off, p)
                pv[pr] = w if pv[pr] is None else pv[pr] + w
        acc = [jnp.exp2(m[pr] - m_new[pr]) * acc[pr] + pv[pr] for pr in pairs]
        return tuple(m_new), tuple(acc), tuple(cm_next)

    carry = (tuple(m), tuple(acc), cm0)
    _, acc, _ = lax.fori_loop(0, lax.div(i + (grp - 1), grp), step, carry)
    for pr in pairs:
        out = acc[pr][:d] * (1.0 / acc[pr][d:d + 1])
        o_ref[0, 2 * d * pr:2 * d * (pr + 1)] = jnp.concatenate([out[:, :t], out[:, t:]], axis=0).astype(BF16)


def _moba_attention(q, k, v, sel):
    B, W, S = q.shape
    t = ATT_TILE
    n_blk = S // t
    assert n_blk % MOBA_GROUP == 0 and N_SELF_HEADS % (2 * MOBA_PAIRS) == 0
    rows = 2 * HEAD_DIM * MOBA_PAIRS
    q_tile = pl.BlockSpec((1, rows, t), lambda b, hq, i: (b, hq, i))
    resident = dict(pipeline_mode=pl.Buffered(1))
    return pl.pallas_call(
        _moba_body,
        out_shape=jax.ShapeDtypeStruct((B, W, S), BF16),
        grid=(B, N_SELF_HEADS // (2 * MOBA_PAIRS), n_blk),
        in_specs=[q_tile,
                  pl.BlockSpec((1, MOBA_PAIRS, S, 2 * HEAD_DIM), lambda b, hq, i: (b, hq, 0, 0), **resident),
                  pl.BlockSpec((1, rows, S), lambda b, hq, i: (b, hq, 0), **resident),
                  pl.BlockSpec((1, 2 * MOBA_PAIRS, n_blk, t), lambda b, hq, i: (b, hq, 0, i))],
        out_specs=q_tile,
        scratch_shapes=([pltpu.VMEM((n_blk, 2 * t * MOBA_PAIRS), F32)]
                        + [pltpu.VMEM((t, 2 * t), F32)] * (MOBA_GROUP * MOBA_PAIRS)),
        compiler_params=pltpu.CompilerParams(dimension_semantics=("arbitrary",) * 3, vmem_limit_bytes=VMEM_LIMIT),
        name="moba_attn",
    )(q, k, v, sel)


def _sb_body(q_ref, k_ref, v_ref, o_ref):
    t, d, pairs = ATT_TILE, HEAD_DIM, range(SB_PAIRS)
    i = pl.program_id(2)
    qq = [_pair_queries(q_ref[0, 2 * d * pr:2 * d * (pr + 1)]) for pr in pairs]
    key = lax.broadcasted_iota(jnp.int32, (t, t), 0)
    qry = lax.broadcasted_iota(jnp.int32, (t, t), 1)
    tri = (qry >= key).astype(BF16)
    tri2 = jnp.concatenate([tri, tri], axis=1)
    causal = jnp.concatenate([key < qry] * 2, axis=1)

    def scores(pr, off):
        return jnp.dot(k_ref[0, pr, pl.ds(off, t), :], qq[pr], preferred_element_type=F32)

    def suffix_sums(zn, masked):
        log_1m = jnp.minimum(zn, 0.0) - jnp.log(1.0 + jnp.exp2(-jnp.abs(zn))) * LOG2_E
        if masked:
            log_1m = jnp.where(causal, log_1m, 0.0)
        hi = log_1m.astype(BF16)
        lo = (log_1m - hi.astype(F32)).astype(BF16)
        return jnp.dot(tri2, jnp.concatenate([hi, lo], axis=0), preferred_element_type=F32)

    def weighted_values(pr, off, a):
        out = []
        for hd in range(2):
            r0 = (2 * pr + hd) * d
            out.append(jnp.dot(v_ref[0, r0:r0 + d, pl.ds(off, t)], a[:, hd * t:(hd + 1) * t],
                               preferred_element_type=F32))
        return jnp.concatenate(out, axis=1)

    off0 = pl.multiple_of(i * t, t)
    off1 = pl.multiple_of(jnp.maximum(i - 1, 0) * t, t)
    has_prev = i >= 1
    z0 = [scores(pr, off0) for pr in pairs]
    z1 = [scores(pr, off1) for pr in pairs]
    incl0 = [suffix_sums(z0[pr], True) for pr in pairs]
    incl1 = [suffix_sums(z1[pr], False) for pr in pairs]
    carry, acc = [], []
    for pr in pairs:
        a0 = jnp.where(causal, jnp.exp2(incl0[pr] - z0[pr]), 0.0)
        carry0 = incl0[pr][0:1, :]
        a1 = jnp.exp2((incl1[pr] + (carry0 + jnp.where(has_prev, 0.0, NEG_BIAS))) - z1[pr])
        acc.append(weighted_values(pr, off0, a0.astype(BF16)) + weighted_values(pr, off1, a1.astype(BF16)))
        carry.append(carry0 + jnp.where(has_prev, incl1[pr][0:1, :], 0.0))

    def liveness(carry):
        live = jnp.max(carry[0])
        for pr in pairs[1:]:
            live = jnp.maximum(live, jnp.max(carry[pr]))
        return live

    def alive(state):
        j, _, _, live = state
        return jnp.logical_and(j >= 0, live > SB_DEAD_LOG2)

    def step(state):
        j, carry, acc, _ = state
        off = pl.multiple_of(j * t, t)
        carry, acc = list(carry), list(acc)
        z = [scores(pr, off) for pr in pairs]
        for pr in pairs:
            incl = suffix_sums(z[pr], False)
            acc[pr] = acc[pr] + weighted_values(pr, off, jnp.exp2((incl + carry[pr]) - z[pr]).astype(BF16))
            carry[pr] = carry[pr] + incl[0:1, :]
        return j - 1, tuple(carry), tuple(acc), liveness(carry)

    _, _, acc, _ = lax.while_loop(alive, step, (i - 2, tuple(carry), tuple(acc), liveness(carry)))
    for pr in pairs:
        o_ref[0, 2 * d * pr:2 * d * (pr + 1)] = jnp.concatenate([acc[pr][:, :t], acc[pr][:, t:]], axis=0).astype(BF16)


def _sb_attention(q, k, v):
    B, W, S = q.shape
    t = ATT_TILE
    assert N_SELF_HEADS % (2 * SB_PAIRS) == 0
    rows = 2 * HEAD_DIM * SB_PAIRS
    q_tile = pl.BlockSpec((1, rows, t), lambda b, hq, i: (b, hq, i))
    resident = dict(pipeline_mode=pl.Buffered(1))
    return pl.pallas_call(
        _sb_body,
        out_shape=jax.ShapeDtypeStruct((B, W, S), BF16),
        grid=(B, N_SELF_HEADS // (2 * SB_PAIRS), S // t),
        in_specs=[q_tile,
                  pl.BlockSpec((1, SB_PAIRS, S, 2 * HEAD_DIM), lambda b, hq, i: (b, hq, 0, 0), **resident),
                  pl.BlockSpec((1, rows, S), lambda b, hq, i: (b, hq, 0), **resident)],
        out_specs=q_tile,
        compiler_params=pltpu.CompilerParams(dimension_semantics=("arbitrary",) * 3, vmem_limit_bytes=VMEM_LIMIT),
        name="sb_attn",
    )(q, k, v)


def kernel(x, mem, positions, g_ffn_pre, w_pre_gate, w_pre_up, w_pre_down, g_mix, w_in, g_mem, w_mem_kv, w_out,
           g_ffn_post, w_post_gate, w_post_up, w_post_down, g_final):
    depth = w_in.shape[0]
    inv_freq = ROPE_THETA ** (-jnp.arange(0, 2 * ROT_HALF, 2, dtype=F32) / (2 * ROT_HALF))
    h = x
    for layer in range(depth):
        h = _ffn(h, g_ffn_pre[layer], w_pre_gate[layer].astype(BF16), w_pre_up[layer].astype(BF16),
                 w_pre_down[layer].astype(BF16))
        kvm = _memkv(mem, g_mem[layer], w_mem_kv[layer].T.astype(BF16))
        w_in_t = w_in[layer].T.astype(BF16)
        if layer % 2 == 0:
            q, k, v, o_mem, sel = _proj(h, g_mix[layer], w_in_t, kvm, rope=(positions, inv_freq))
            o_self = _moba_attention(q, k, v, sel)
        else:
            q, k, v, o_mem = _proj(h, g_mix[layer], w_in_t, kvm)
            o_self = _sb_attention(q, k, v)
        w_o = w_out[layer].astype(BF16)
        h = _ffn(h, g_ffn_post[layer], w_post_gate[layer].astype(BF16), w_post_up[layer].astype(BF16),
                 w_post_down[layer].astype(BF16),
                 attn=(o_self, o_mem, w_o[:SELF_WIDTH], w_o[SELF_WIDTH:]),
                 final_g=g_final if layer == depth - 1 else None)
    return h
```

```python
import functools
import math

import numpy as np
import jax
import jax.numpy as jnp
from jax import lax
from jax.experimental import pallas as pl
from jax.experimental.pallas import tpu as pltpu

F32 = jnp.float32
BF16 = jnp.bfloat16

HEAD_DIM = 64
N_SELF_HEADS = 12
N_MEM_HEADS = 4
SELF_WIDTH = N_SELF_HEADS * HEAD_DIM
MEM_WIDTH = N_MEM_HEADS * HEAD_DIM
ROT_HALF = 8
ROPE_THETA = 500000.0
MOBA_BLOCK = 256
MOBA_TOPK = 3
RMS_EPS = 1e-6
NEG_BIAS = -1e30
QK_SCALE = 1.0 / math.sqrt(HEAD_DIM)
LOG2_E = math.log2(math.e)
ONES_ROWS = 16

FFN_ROWS = 512
FFN_CHUNK = 256
PROJ_SUBTILES = 2
ATT_TILE = 256
MOBA_GROUP = 4
MOBA_PAIRS = 3
SB_PAIRS = 3
SB_DEAD_LOG2 = -160.0
VMEM_LIMIT = 56 * 1024 * 1024

_TWO_PI = 2.0 * math.pi
_C1 = 6.28125
_C2 = round((_TWO_PI - _C1) * 2 ** 20) / 2 ** 20
_C3 = float(np.float32(_TWO_PI - _C1 - _C2))
_INV_TWO_PI = float(np.float32(1.0 / _TWO_PI))


def _rms_scale(x):
    return x * lax.rsqrt(jnp.mean(x * x, axis=-1, keepdims=True) + RMS_EPS)


def _dot_t(a, b, **kw):
    return lax.dot_general(a, b, (((0,), (0,)), ((), ())), preferred_element_type=F32, **kw)


def _dot_nt(a, b):
    return lax.dot_general(a, b, (((1,), (1,)), ((), ())), preferred_element_type=F32)


def _ffn_body(*refs, n_chunks, with_attn, with_final):
    it = iter(refs)
    x_ref = next(it)
    if with_attn:
        os_ref, om_ref, wos_ref, wom_ref = next(it), next(it), next(it), next(it)
    g_ref, wg_ref, wu_ref, wd_ref = next(it), next(it), next(it), next(it)
    gf_ref = next(it) if with_final else None
    o_ref, x_scr, u_scr, acc_scr = next(it), next(it), next(it), next(it)

    x = x_ref[...]
    if with_attn:
        x = x + _dot_t(os_ref[0], wos_ref[...]) + _dot_t(om_ref[0], wom_ref[...])
    x_scr[...] = x
    u_scr[...] = (_rms_scale(x) * g_ref[...]).astype(BF16)
    acc_scr[...] = jnp.zeros_like(acc_scr)

    for c in range(n_chunks):
        cols = slice(c * FFN_CHUNK, (c + 1) * FFN_CHUNK)
        u = u_scr[...]
        gate = jnp.dot(u, wg_ref[:, cols], preferred_element_type=F32)
        up = jnp.dot(u, wu_ref[:, cols], preferred_element_type=F32)
        act = (gate * jax.nn.sigmoid(gate)) * up
        acc_scr[...] += jnp.dot(act.astype(BF16), wd_ref[cols, :], preferred_element_type=F32)
    y = x_scr[...] + 0.5 * acc_scr[...]
    if with_final:
        y = _rms_scale(y) * gf_ref[...]
    o_ref[...] = y


def _ffn(h, g, wg, wu, wd, attn=None, final_g=None):
    B, S, D = h.shape
    assert wg.shape[1] % FFN_CHUNK == 0
    n_chunks = wg.shape[1] // FFN_CHUNK
    tm = FFN_ROWS
    assert S % tm == 0
    grid = (B, S // tm)
    const2 = lambda b, i: (0, 0)
    resident = dict(pipeline_mode=pl.Buffered(1))
    in_specs = [pl.BlockSpec((None, tm, D), lambda b, i: (b, i, 0))]
    args = [h]
    if attn is not None:
        o_self, o_mem, w_os, w_om = attn
        in_specs += [
            pl.BlockSpec((1, SELF_WIDTH, tm), lambda b, i: (b, 0, i)),
            pl.BlockSpec((1, MEM_WIDTH, tm), lambda b, i: (b, 0, i)),
            pl.BlockSpec(w_os.shape, const2, **resident),
            pl.BlockSpec(w_om.shape, const2, **resident),
        ]
        args += [o_self, o_mem, w_os, w_om]
    in_specs += [
        pl.BlockSpec((1, D), const2, **resident),
        pl.BlockSpec(wg.shape, const2, **resident),
        pl.BlockSpec(wu.shape, const2, **resident),
        pl.BlockSpec(wd.shape, const2, **resident),
    ]
    args += [g.reshape(1, D), wg, wu, wd]
    if final_g is not None:
        in_specs.append(pl.BlockSpec((1, D), const2, **resident))
        args.append(final_g.reshape(1, D))
    body = functools.partial(_ffn_body, n_chunks=n_chunks, with_attn=attn is not None,
                             with_final=final_g is not None)
    return pl.pallas_call(
        body,
        out_shape=jax.ShapeDtypeStruct((B, S, D), F32),
        grid=grid,
        in_specs=in_specs,
        out_specs=pl.BlockSpec((None, tm, D), lambda b, i: (b, i, 0)),
        scratch_shapes=[pltpu.VMEM((tm, D), F32), pltpu.VMEM((tm, D), BF16), pltpu.VMEM((tm, D), F32)],
        compiler_params=pltpu.CompilerParams(dimension_semantics=("arbitrary", "arbitrary"),
                                             vmem_limit_bytes=VMEM_LIMIT),
        name="ffn_attn" if attn is not None else "ffn",
    )(*args)


def _memkv_body(mem_ref, g_ref, w_ref, o_ref):
    u = (_rms_scale(mem_ref[0]) * g_ref[...]).astype(BF16)
    o_ref[0] = _dot_nt(w_ref[...], u).astype(BF16)


def _memkv(mem, g_mem, w_kv_t):
    B, n_mem, D = mem.shape
    return pl.pallas_call(
        _memkv_body,
        out_shape=jax.ShapeDtypeStruct((B, 2 * MEM_WIDTH, n_mem), BF16),
        grid=(B,),
        in_specs=[pl.BlockSpec((1, n_mem, D), lambda b: (b, 0, 0)),
                  pl.BlockSpec((1, D), lambda b: (0, 0)),
                  pl.BlockSpec(w_kv_t.shape, lambda b: (0, 0))],
        out_specs=pl.BlockSpec((1, 2 * MEM_WIDTH, n_mem), lambda b: (b, 0, 0)),
        name="memkv",
    )(mem, g_mem.reshape(1, -1), w_kv_t)


def _memory_attention(p_scr, kvm_ref, om_ref, lanes):
    heads = range(N_MEM_HEADS)
    s = []
    for hm in heads:
        r0 = 3 * SELF_WIDTH + hm * HEAD_DIM
        qm = (p_scr[r0:r0 + HEAD_DIM, :] * QK_SCALE).astype(BF16)
        km = kvm_ref[0, hm * HEAD_DIM:(hm + 1) * HEAD_DIM, :]
        s.append(_dot_t(km, qm))
    p = [jnp.exp(s[hm] - jnp.max(s[hm], axis=0, keepdims=True)) for hm in heads]
    for hm in heads:
        vm = kvm_ref[0, MEM_WIDTH + hm * HEAD_DIM:MEM_WIDTH + (hm + 1) * HEAD_DIM, :]
        l = jnp.sum(p[hm], axis=0, keepdims=True)
        o = jnp.dot(vm, p[hm].astype(BF16), preferred_element_type=F32)
        om_ref[0, hm * HEAD_DIM:(hm + 1) * HEAD_DIM, lanes] = (o * (1.0 / l)).astype(BF16)


def _rotary_and_gate(p_scr, own, pos, freq_ref, sel_ref, km_scr, lanes):
    ang = freq_ref[...] * pos.astype(F32)
    n = jnp.floor(ang * _INV_TWO_PI + 0.5)
    red = ((ang - n * _C1) - n * _C2) - n * _C3
    cos, sin = jnp.cos(red), jnp.sin(red)
    for base in range(0, 2 * SELF_WIDTH, HEAD_DIM):
        x1 = p_scr[base:base + ROT_HALF, :]
        x2 = p_scr[base + ROT_HALF:base + 2 * ROT_HALF, :]
        p_scr[base:base + ROT_HALF, :] = x1 * cos - x2 * sin
        p_scr[base + ROT_HALF:base + 2 * ROT_HALF, :] = x2 * cos + x1 * sin

    tile = p_scr.shape[1]
    n_blk = km_scr.shape[1]
    blk_lane = lax.broadcasted_iota(jnp.int32, (HEAD_DIM, n_blk), 1)
    blk = lax.broadcasted_iota(jnp.int32, (n_blk, tile), 0)
    gates = [_dot_t(km_scr[hd * HEAD_DIM:(hd + 1) * HEAD_DIM, :], p_scr[hd * HEAD_DIM:(hd + 1) * HEAD_DIM, :] * QK_SCALE,
                    precision=lax.Precision.HIGHEST) for hd in range(N_SELF_HEADS)]
    for hd in range(N_SELF_HEADS):
        r0 = hd * HEAD_DIM
        gate = jnp.where(blk < own, gates[hd], -jnp.inf)
        for _ in range(MOBA_TOPK):
            top = jnp.max(gate, axis=0, keepdims=True)
            first = jnp.min(jnp.where(gate == top, blk, n_blk), axis=0, keepdims=True)
            gate = jnp.where(blk == first, -jnp.inf, gate)
        picked = jnp.logical_and(gate == -jnp.inf, blk < own)
        sel_ref[0, hd, :, lanes] = jnp.where(picked, 0.0, -jnp.inf).astype(BF16)
        kmean = jnp.mean(p_scr[SELF_WIDTH + r0:SELF_WIDTH + r0 + HEAD_DIM, :], axis=1, keepdims=True)
        km_scr[r0:r0 + HEAD_DIM, :] = jnp.where(blk_lane == own, kmean, km_scr[r0:r0 + HEAD_DIM, :])


def _proj_body(*refs, rotary_gate):
    it = iter(refs)
    h_ref, g_ref, w_ref, kvm_ref = next(it), next(it), next(it), next(it)
    if rotary_gate:
        pos_ref, freq_ref = next(it), next(it)
    q_ref, k_ref, v_ref, om_ref = next(it), next(it), next(it), next(it)
    if rotary_gate:
        sel_ref = next(it)
    p_scrs = [next(it) for _ in range(PROJ_SUBTILES)]
    if rotary_gate:
        km_scr = next(it)
    tile = MOBA_BLOCK

    for sub, p_scr in enumerate(p_scrs):
        u = (_rms_scale(h_ref[0, sub * tile:(sub + 1) * tile, :]) * g_ref[...]).astype(BF16)
        p_scr[...] = _dot_nt(w_ref[...], u)

    if rotary_gate:
        @pl.when(pl.program_id(1) == 0)
        def _():
            km_scr[...] = jnp.zeros_like(km_scr)

    for sub, p_scr in enumerate(p_scrs):
        lanes = slice(sub * tile, (sub + 1) * tile)
        if rotary_gate:
            own = pl.program_id(1) * PROJ_SUBTILES + sub
            _rotary_and_gate(p_scr, own, pos_ref[0, :, lanes], freq_ref, sel_ref, km_scr, lanes)
        q_scale = QK_SCALE * LOG2_E if rotary_gate else -QK_SCALE * LOG2_E
        q_ref[0, :, lanes] = (p_scr[0:SELF_WIDTH, :] * q_scale).astype(BF16)
        for hp in range(N_SELF_HEADS // 2):
            r0 = SELF_WIDTH + hp * 2 * HEAD_DIM
            k_ref[0, hp, lanes, :] = p_scr[r0:r0 + 2 * HEAD_DIM, :].T.astype(BF16)
        v_ref[0, :, lanes] = p_scr[2 * SELF_WIDTH:3 * SELF_WIDTH, :].astype(BF16)
        _memory_attention(p_scr, kvm_ref, om_ref, lanes)


def _proj(h, g, w_in_t, kvm, rope=None):
    B, S, D = h.shape
    pw = w_in_t.shape[0]
    tm = MOBA_BLOCK * PROJ_SUBTILES
    assert S % tm == 0
    n_blk = S // MOBA_BLOCK
    grid = (B, S // tm)
    const2 = lambda b, i: (0, 0)
    in_specs = [pl.BlockSpec((1, tm, D), lambda b, i: (b, i, 0)),
                pl.BlockSpec((1, D), const2),
                pl.BlockSpec(w_in_t.shape, const2, pipeline_mode=pl.Buffered(1)),
                pl.BlockSpec((1,) + kvm.shape[1:], lambda b, i: (b, 0, 0))]
    args = [h, g.reshape(1, D), w_in_t, kvm]
    slab = lambda rows: pl.BlockSpec((1, rows, tm), lambda b, i: (b, 0, i))
    n_pairs = N_SELF_HEADS // 2
    out_shape = [jax.ShapeDtypeStruct((B, SELF_WIDTH, S), BF16),
                 jax.ShapeDtypeStruct((B, n_pairs, S, 2 * HEAD_DIM), BF16),
                 jax.ShapeDtypeStruct((B, SELF_WIDTH, S), BF16),
                 jax.ShapeDtypeStruct((B, MEM_WIDTH, S), BF16)]
    out_specs = [slab(SELF_WIDTH),
                 pl.BlockSpec((1, n_pairs, tm, 2 * HEAD_DIM), lambda b, i: (b, 0, i, 0)),
                 slab(SELF_WIDTH), slab(MEM_WIDTH)]
    scratch = [pltpu.VMEM((pw, MOBA_BLOCK), F32)] * PROJ_SUBTILES
    if rope is not None:
        positions, inv_freq = rope
        in_specs += [pl.BlockSpec((1, 1, tm), lambda b, i: (b, 0, i)),
                     pl.BlockSpec((ROT_HALF, 1), const2)]
        args += [positions.reshape(B, 1, S), inv_freq.reshape(ROT_HALF, 1)]
        out_shape.append(jax.ShapeDtypeStruct((B, N_SELF_HEADS, n_blk, S), BF16))
        out_specs.append(pl.BlockSpec((1, N_SELF_HEADS, n_blk, tm), lambda b, i: (b, 0, 0, i)))
        scratch.append(pltpu.VMEM((SELF_WIDTH, n_blk), F32))
    return pl.pallas_call(
        functools.partial(_proj_body, rotary_gate=rope is not None),
        out_shape=out_shape,
        grid=grid,
        in_specs=in_specs,
        out_specs=out_specs,
        scratch_shapes=scratch,
        compiler_params=pltpu.CompilerParams(dimension_semantics=("arbitrary", "arbitrary"),
                                             vmem_limit_bytes=VMEM_LIMIT),
        name="proj_moba" if rope is not None else "proj_sb",
    )(*args)


def _pair_queries(q):
    d, t = HEAD_DIM, q.shape[1]
    zero = jnp.zeros((d, t), q.dtype)
    return jnp.concatenate([jnp.concatenate([q[:d], zero], axis=1), jnp.concatenate([zero, q[d:]], axis=1)], axis=0)


def _moba_body(q_ref, k_ref, v_ref, sel_ref, o_ref, bias_scr, *s_scr):
    t, grp, d, pairs = ATT_TILE, MOBA_GROUP, HEAD_DIM, range(MOBA_PAIRS)
    i = pl.program_id(2)
    qq = [_pair_queries(q_ref[0, 2 * d * pr:2 * d * (pr + 1)]) for pr in pairs]
    bias_scr[...] = jnp.concatenate([sel_ref[0, hd] for hd in range(2 * MOBA_PAIRS)], axis=1).astype(F32)
    ones = jnp.ones((ONES_ROWS, t), BF16)
    last_group = bias_scr.shape[0] // grp - 1

    def bias(j, pr):
        return bias_scr[pl.ds(j, 1), 2 * t * pr:2 * t * (pr + 1)]

    def weighted_values(pr, off, p):
        out = []
        for hd in range(2):
            r0 = (2 * pr + hd) * d
            v1 = jnp.concatenate([v_ref[0, r0:r0 + d, pl.ds(off, t)], ones], axis=0)
            out.append(jnp.dot(v1, p[:, hd * t:(hd + 1) * t], preferred_element_type=F32))
        return jnp.concatenate(out, axis=1)

    def score_block(jg, g, pr):
        off = pl.multiple_of((jg * grp + g) * t, t)
        s = jnp.dot(k_ref[0, pr, pl.ds(off, t), :], qq[pr], preferred_element_type=F32)
        s_scr[g * MOBA_PAIRS + pr][...] = s
        return jnp.max(s, axis=0, keepdims=True) + bias(jg * grp + g, pr)

    def score_group(jg):
        cm = [None] * MOBA_PAIRS
        for g in range(grp):
            for pr in pairs:
                c = score_block(jg, g, pr)
                cm[pr] = c if cm[pr] is None else jnp.maximum(cm[pr], c)
        return tuple(cm)

    off_i = pl.multiple_of(i * t, t)
    key = lax.broadcasted_iota(jnp.int32, (t, t), 0)
    qry = lax.broadcasted_iota(jnp.int32, (t, t), 1)
    causal = jnp.concatenate([key <= qry] * 2, axis=1)
    s_own = [jnp.dot(k_ref[0, pr, pl.ds(off_i, t), :], qq[pr], preferred_element_type=F32) for pr in pairs]
    cm0 = score_group(0)
    s_own = [jnp.where(causal, s_own[pr], -jnp.inf) for pr in pairs]
    m = [jnp.max(s_own[pr], axis=0, keepdims=True) for pr in pairs]
    acc = [weighted_values(pr, off_i, jnp.exp2(s_own[pr] - m[pr]).astype(BF16)) for pr in pairs]

    def step(n, carry):
        m, acc, cm = carry
        m_new = [jnp.maximum(m[pr], cm[pr]) for pr in pairs]
        nxt = jnp.minimum(n + 1, last_group)
        pv, cm_next = [None] * MOBA_PAIRS, [None] * MOBA_PAIRS
        for g in range(grp):
            off = pl.multiple_of((n * grp + g) * t, t)
            for pr in pairs:
                s = s_scr[g * MOBA_PAIRS + pr][...]
                c = score_block(nxt, g, pr)
                cm_next[pr] = c if cm_next[pr] is None else jnp.maximum(cm_next[pr], c)
                p = jnp.exp2(s + (bias(n * grp + g, pr) - m_new[pr])).astype(BF16)
                w = weighted_values(pr, off, p)
                pv[pr] = w if pv[pr] is None else pv[pr] + w
        acc = [jnp.exp2(m[pr] - m_new[pr]) * acc[pr] + pv[pr] for pr in pairs]
        return tuple(m_new), tuple(acc), tuple(cm_next)

    carry = (tuple(m), tuple(acc), cm0)
    _, acc, _ = lax.fori_loop(0, lax.div(i + (grp - 1), grp), step, carry)
    for pr in pairs:
        out = acc[pr][:d] * (1.0 / acc[pr][d:d + 1])
        o_ref[0, 2 * d * pr:2 * d * (pr + 1)] = jnp.concatenate([out[:, :t], out[:, t:]], axis=0).astype(BF16)


def _moba_attention(q, k, v, sel):
    B, W, S = q.shape
    t = ATT_TILE
    n_blk = S // t
    assert n_blk % MOBA_GROUP == 0 and N_SELF_HEADS % (2 * MOBA_PAIRS) == 0
    rows = 2 * HEAD_DIM * MOBA_PAIRS
    q_tile = pl.BlockSpec((1, rows, t), lambda b, hq, i: (b, hq, i))
    resident = dict(pipeline_mode=pl.Buffered(1))
    return pl.pallas_call(
        _moba_body,
        out_shape=jax.ShapeDtypeStruct((B, W, S), BF16),
        grid=(B, N_SELF_HEADS // (2 * MOBA_PAIRS), n_blk),
        in_specs=[q_tile,
                  pl.BlockSpec((1, MOBA_PAIRS, S, 2 * HEAD_DIM), lambda b, hq, i: (b, hq, 0, 0), **resident),
                  pl.BlockSpec((1, rows, S), lambda b, hq, i: (b, hq, 0), **resident),
                  pl.BlockSpec((1, 2 * MOBA_PAIRS, n_blk, t), lambda b, hq, i: (b, hq, 0, i))],
        out_specs=q_tile,
        scratch_shapes=([pltpu.VMEM((n_blk, 2 * t * MOBA_PAIRS), F32)]
                        + [pltpu.VMEM((t, 2 * t), F32)] * (MOBA_GROUP * MOBA_PAIRS)),
        compiler_params=pltpu.CompilerParams(dimension_semantics=("arbitrary",) * 3, vmem_limit_bytes=VMEM_LIMIT),
        name="moba_attn",
    )(q, k, v, sel)


def _sb_body(q_ref, k_ref, v_ref, o_ref):
    t, d, pairs = ATT_TILE, HEAD_DIM, range(SB_PAIRS)
    i = pl.program_id(2)
    qq = [_pair_queries(q_ref[0, 2 * d * pr:2 * d * (pr + 1)]) for pr in pairs]
    key = lax.broadcasted_iota(jnp.int32, (t, t), 0)
    qry = lax.broadcasted_iota(jnp.int32, (t, t), 1)
    tri = (qry >= key).astype(BF16)
    tri2 = jnp.concatenate([tri, tri], axis=1)
    causal = jnp.concatenate([key < qry] * 2, axis=1)

    def scores(pr, off):
        return jnp.dot(k_ref[0, pr, pl.ds(off, t), :], qq[pr], preferred_element_type=F32)

    def suffix_sums(zn):
        log_1m = jnp.minimum(zn, 0.0) - jnp.log(1.0 + jnp.exp2(-jnp.abs(zn))) * LOG2_E
        hi = log_1m.astype(BF16)
        lo = (log_1m - hi.astype(F32)).astype(BF16)
        return jnp.dot(tri2, jnp.concatenate([hi, lo], axis=0), preferred_element_type=F32)

    def weighted_values(pr, off, a):
        out = []
        for hd in range(2):
            r0 = (2 * pr + hd) * d
            out.append(jnp.dot(v_ref[0, r0:r0 + d, pl.ds(off, t)], a[:, hd * t:(hd + 1) * t],
                               preferred_element_type=F32))
        return jnp.concatenate(out, axis=1)

    off0 = pl.multiple_of(i * t, t)
    off1 = pl.multiple_of(jnp.maximum(i - 1, 0) * t, t)
    has_prev = i >= 1
    z0 = [jnp.where(causal, scores(pr, off0), jnp.inf) for pr in pairs]
    z1 = [scores(pr, off1) for pr in pairs]
    incl0 = [suffix_sums(z0[pr]) for pr in pairs]
    incl1 = [suffix_sums(z1[pr]) for pr in pairs]
    carry, acc = [], []
    for pr in pairs:
        a0 = jnp.exp2(incl0[pr] - z0[pr])
        carry0 = incl0[pr][0:1, :]
        a1 = jnp.exp2((incl1[pr] + (carry0 + jnp.where(has_prev, 0.0, NEG_BIAS))) - z1[pr])
        acc.append(weighted_values(pr, off0, a0.astype(BF16)) + weighted_values(pr, off1, a1.astype(BF16)))
        carry.append(carry0 + jnp.where(has_prev, incl1[pr][0:1, :], 0.0))

    def liveness(carry):
        live = jnp.max(carry[0])
        for pr in pairs[1:]:
            live = jnp.maximum(live, jnp.max(carry[pr]))
        return live

    def alive(state):
        j, _, _, live = state
        return jnp.logical_and(j >= 0, live > SB_DEAD_LOG2)

    def step(state):
        j, carry, acc, _ = state
        off = pl.multiple_of(j * t, t)
        carry, acc = list(carry), list(acc)
        z = [scores(pr, off) for pr in pairs]
        for pr in pairs:
            incl = suffix_sums(z[pr])
            acc[pr] = acc[pr] + weighted_values(pr, off, jnp.exp2((incl + carry[pr]) - z[pr]).astype(BF16))
            carry[pr] = carry[pr] + incl[0:1, :]
        return j - 1, tuple(carry), tuple(acc), liveness(carry)

    _, _, acc, _ = lax.while_loop(alive, step, (i - 2, tuple(carry), tuple(acc), liveness(carry)))
    for pr in pairs:
        o_ref[0, 2 * d * pr:2 * d * (pr + 1)] = jnp.concatenate([acc[pr][:, :t], acc[pr][:, t:]], axis=0).astype(BF16)


def _sb_attention(q, k, v):
    B, W, S = q.shape
    t = ATT_TILE
    assert N_SELF_HEADS % (2 * SB_PAIRS) == 0
    rows = 2 * HEAD_DIM * SB_PAIRS
    q_tile = pl.BlockSpec((1, rows, t), lambda b, hq, i: (b, hq, i))
    resident = dict(pipeline_mode=pl.Buffered(1))
    return pl.pallas_call(
        _sb_body,
        out_shape=jax.ShapeDtypeStruct((B, W, S), BF16),
        grid=(B, N_SELF_HEADS // (2 * SB_PAIRS), S // t),
        in_specs=[q_tile,
                  pl.BlockSpec((1, SB_PAIRS, S, 2 * HEAD_DIM), lambda b, hq, i: (b, hq, 0, 0), **resident),
                  pl.BlockSpec((1, rows, S), lambda b, hq, i: (b, hq, 0), **resident)],
        out_specs=q_tile,
        compiler_params=pltpu.CompilerParams(dimension_semantics=("arbitrary",) * 3, vmem_limit_bytes=VMEM_LIMIT),
        name="sb_attn",
    )(q, k, v)


def kernel(x, mem, positions, g_ffn_pre, w_pre_gate, w_pre_up, w_pre_down, g_mix, w_in, g_mem, w_mem_kv, w_out,
           g_ffn_post, w_post_gate, w_post_up, w_post_down, g_final):
    depth = w_in.shape[0]
    inv_freq = ROPE_THETA ** (-jnp.arange(0, 2 * ROT_HALF, 2, dtype=F32) / (2 * ROT_HALF))
    h = x
    for layer in range(depth):
        h = _ffn(h, g_ffn_pre[layer], w_pre_gate[layer].astype(BF16), w_pre_up[layer].astype(BF16),
                 w_pre_down[layer].astype(BF16))
        kvm = _memkv(mem, g_mem[layer], w_mem_kv[layer].T.astype(BF16))
        w_in_t = w_in[layer].T.astype(BF16)
        if layer % 2 == 0:
            q, k, v, o_mem, sel = _proj(h, g_mix[layer], w_in_t, kvm, rope=(positions, inv_freq))
            o_self = _moba_attention(q, k, v, sel)
        else:
            q, k, v, o_mem = _proj(h, g_mix[layer], w_in_t, kvm)
            o_self = _sb_attention(q, k, v)
        w_o = w_out[layer].astype(BF16)
        h = _ffn(h, g_ffn_post[layer], w_post_gate[layer].astype(BF16), w_post_up[layer].astype(BF16),
                 w_post_down[layer].astype(BF16),
                 attn=(o_self, o_mem, w_o[:SELF_WIDTH], w_o[SELF_WIDTH:]),
                 final_g=g_final if layer == depth - 1 else None)
    return h
```

```python
import functools
import math

import numpy as np
import jax
import jax.numpy as jnp
from jax import lax
from jax.experimental import pallas as pl
from jax.experimental.pallas import tpu as pltpu

F32 = jnp.float32
BF16 = jnp.bfloat16

HEAD_DIM = 64
N_SELF_HEADS = 12
N_MEM_HEADS = 4
SELF_WIDTH = N_SELF_HEADS * HEAD_DIM
MEM_WIDTH = N_MEM_HEADS * HEAD_DIM
ROT_HALF = 8
ROPE_THETA = 500000.0
MOBA_BLOCK = 256
MOBA_TOPK = 3
RMS_EPS = 1e-6
NEG_BIAS = -1e30
QK_SCALE = 1.0 / math.sqrt(HEAD_DIM)
LOG2_E = math.log2(math.e)
ONES_ROWS = 16

FFN_ROWS = 512
FFN_CHUNK = 256
PROJ_SUBTILES = 2
ATT_TILE = 256
MOBA_GROUP = 4
MOBA_PAIRS = 3
SB_PAIRS = 3
SB_DEAD_LOG2 = -160.0
VMEM_LIMIT = 56 * 1024 * 1024

_TWO_PI = 2.0 * math.pi
_C1 = 6.28125
_C2 = round((_TWO_PI - _C1) * 2 ** 20) / 2 ** 20
_C3 = float(np.float32(_TWO_PI - _C1 - _C2))
_INV_TWO_PI = float(np.float32(1.0 / _TWO_PI))


def _rms_scale(x):
    return x * lax.rsqrt(jnp.mean(x * x, axis=-1, keepdims=True) + RMS_EPS)


def _dot_t(a, b, **kw):
    return lax.dot_general(a, b, (((0,), (0,)), ((), ())), preferred_element_type=F32, **kw)


def _dot_nt(a, b):
    return lax.dot_general(a, b, (((1,), (1,)), ((), ())), preferred_element_type=F32)


def _ffn_body(*refs, n_chunks, with_attn, with_final):
    it = iter(refs)
    x_ref = next(it)
    if with_attn:
        os_ref, om_ref, wos_ref, wom_ref = next(it), next(it), next(it), next(it)
    g_ref, wg_ref, wu_ref, wd_ref = next(it), next(it), next(it), next(it)
    gf_ref = next(it) if with_final else None
    o_ref, x_scr, u_scr, acc_scr = next(it), next(it), next(it), next(it)

    x = x_ref[...]
    if with_attn:
        x = x + _dot_t(os_ref[0], wos_ref[...]) + _dot_t(om_ref[0], wom_ref[...])
    x_scr[...] = x
    u_scr[...] = (_rms_scale(x) * g_ref[...]).astype(BF16)
    acc_scr[...] = jnp.zeros_like(acc_scr)

    for c in range(n_chunks):
        cols = slice(c * FFN_CHUNK, (c + 1) * FFN_CHUNK)
        u = u_scr[...]
        gate = jnp.dot(u, wg_ref[:, cols], preferred_element_type=F32)
        up = jnp.dot(u, wu_ref[:, cols], preferred_element_type=F32)
        act = (gate * jax.nn.sigmoid(gate)) * up
        acc_scr[...] += jnp.dot(act.astype(BF16), wd_ref[cols, :], preferred_element_type=F32)
    y = x_scr[...] + 0.5 * acc_scr[...]
    if with_final:
        y = _rms_scale(y) * gf_ref[...]
    o_ref[...] = y


def _ffn(h, g, wg, wu, wd, attn=None, final_g=None):
    B, S, D = h.shape
    assert wg.shape[1] % FFN_CHUNK == 0
    n_chunks = wg.shape[1] // FFN_CHUNK
    tm = FFN_ROWS
    assert S % tm == 0
    grid = (B, S // tm)
    const2 = lambda b, i: (0, 0)
    resident = dict(pipeline_mode=pl.Buffered(1))
    in_specs = [pl.BlockSpec((None, tm, D), lambda b, i: (b, i, 0))]
    args = [h]
    if attn is not None:
        o_self, o_mem, w_os, w_om = attn
        in_specs += [
            pl.BlockSpec((1, SELF_WIDTH, tm), lambda b, i: (b, 0, i)),
            pl.BlockSpec((1, MEM_WIDTH, tm), lambda b, i: (b, 0, i)),
            pl.BlockSpec(w_os.shape, const2, **resident),
            pl.BlockSpec(w_om.shape, const2, **resident),
        ]
        args += [o_self, o_mem, w_os, w_om]
    in_specs += [
        pl.BlockSpec((1, D), const2, **resident),
        pl.BlockSpec(wg.shape, const2, **resident),
        pl.BlockSpec(wu.shape, const2, **resident),
        pl.BlockSpec(wd.shape, const2, **resident),
    ]
    args += [g.reshape(1, D), wg, wu, wd]
    if final_g is not None:
        in_specs.append(pl.BlockSpec((1, D), const2, **resident))
        args.append(final_g.reshape(1, D))
    body = functools.partial(_ffn_body, n_chunks=n_chunks, with_attn=attn is not None,
                             with_final=final_g is not None)
    return pl.pallas_call(
        body,
        out_shape=jax.ShapeDtypeStruct((B, S, D), F32),
        grid=grid,
        in_specs=in_specs,
        out_specs=pl.BlockSpec((None, tm, D), lambda b, i: (b, i, 0)),
        scratch_shapes=[pltpu.VMEM((tm, D), F32), pltpu.VMEM((tm, D), BF16), pltpu.VMEM((tm, D), F32)],
        compiler_params=pltpu.CompilerParams(dimension_semantics=("arbitrary", "arbitrary"),
                                             vmem_limit_bytes=VMEM_LIMIT),
        name="ffn_attn" if attn is not None else "ffn",
    )(*args)


def _memkv_body(mem_ref, g_ref, w_ref, o_ref):
    u = (_rms_scale(mem_ref[0]) * g_ref[...]).astype(BF16)
    o_ref[0] = _dot_nt(w_ref[...], u).astype(BF16)


def _memkv(mem, g_mem, w_kv_t):
    B, n_mem, D = mem.shape
    return pl.pallas_call(
        _memkv_body,
        out_shape=jax.ShapeDtypeStruct((B, 2 * MEM_WIDTH, n_mem), BF16),
        grid=(B,),
        in_specs=[pl.BlockSpec((1, n_mem, D), lambda b: (b, 0, 0)),
                  pl.BlockSpec((1, D), lambda b: (0, 0)),
                  pl.BlockSpec(w_kv_t.shape, lambda b: (0, 0))],
        out_specs=pl.BlockSpec((1, 2 * MEM_WIDTH, n_mem), lambda b: (b, 0, 0)),
        name="memkv",
    )(mem, g_mem.reshape(1, -1), w_kv_t)


def _memory_attention(p_scr, kvm_ref, om_ref, lanes):
    heads = range(N_MEM_HEADS)
    s = []
    for hm in heads:
        r0 = 3 * SELF_WIDTH + hm * HEAD_DIM
        qm = (p_scr[r0:r0 + HEAD_DIM, :] * QK_SCALE).astype(BF16)
        km = kvm_ref[0, hm * HEAD_DIM:(hm + 1) * HEAD_DIM, :]
        s.append(_dot_t(km, qm))
    p = [jnp.exp(s[hm] - jnp.max(s[hm], axis=0, keepdims=True)) for hm in heads]
    for hm in heads:
        vm = kvm_ref[0, MEM_WIDTH + hm * HEAD_DIM:MEM_WIDTH + (hm + 1) * HEAD_DIM, :]
        l = jnp.sum(p[hm], axis=0, keepdims=True)
        o = jnp.dot(vm, p[hm].astype(BF16), preferred_element_type=F32)
        om_ref[0, hm * HEAD_DIM:(hm + 1) * HEAD_DIM, lanes] = (o * (1.0 / l)).astype(BF16)


def _rotary_and_gate(p_scr, own, pos, freq_ref, sel_ref, km_scr, lanes):
    ang = freq_ref[...] * pos.astype(F32)
    n = jnp.floor(ang * _INV_TWO_PI + 0.5)
    red = ((ang - n * _C1) - n * _C2) - n * _C3
    cos, sin = jnp.cos(red), jnp.sin(red)
    for base in range(0, 2 * SELF_WIDTH, HEAD_DIM):
        x1 = p_scr[base:base + ROT_HALF, :]
        x2 = p_scr[base + ROT_HALF:base + 2 * ROT_HALF, :]
        p_scr[base:base + ROT_HALF, :] = x1 * cos - x2 * sin
        p_scr[base + ROT_HALF:base + 2 * ROT_HALF, :] = x2 * cos + x1 * sin

    tile = p_scr.shape[1]
    n_blk = km_scr.shape[1]
    blk_lane = lax.broadcasted_iota(jnp.int32, (HEAD_DIM, n_blk), 1)
    blk = lax.broadcasted_iota(jnp.int32, (n_blk, tile), 0)
    gates = [_dot_t(km_scr[hd * HEAD_DIM:(hd + 1) * HEAD_DIM, :], p_scr[hd * HEAD_DIM:(hd + 1) * HEAD_DIM, :] * QK_SCALE,
                    precision=lax.Precision.HIGHEST) for hd in range(N_SELF_HEADS)]
    for hd in range(N_SELF_HEADS):
        r0 = hd * HEAD_DIM
        gate = jnp.where(blk < own, gates[hd], -jnp.inf)
        for _ in range(MOBA_TOPK):
            top = jnp.max(gate, axis=0, keepdims=True)
            first = jnp.min(jnp.where(gate == top, blk, n_blk), axis=0, keepdims=True)
            gate = jnp.where(blk == first, -jnp.inf, gate)
        picked = jnp.logical_and(gate == -jnp.inf, blk < own)
        sel_ref[0, hd, :, lanes] = jnp.where(picked, 0.0, -jnp.inf).astype(BF16)
        kmean = jnp.mean(p_scr[SELF_WIDTH + r0:SELF_WIDTH + r0 + HEAD_DIM, :], axis=1, keepdims=True)
        km_scr[r0:r0 + HEAD_DIM, :] = jnp.where(blk_lane == own, kmean, km_scr[r0:r0 + HEAD_DIM, :])


def _proj_body(*refs, rotary_gate):
    it = iter(refs)
    h_ref, g_ref, w_ref, kvm_ref = next(it), next(it), next(it), next(it)
    if rotary_gate:
        pos_ref, freq_ref = next(it), next(it)
    q_ref, k_ref, v_ref, om_ref = next(it), next(it), next(it), next(it)
    if rotary_gate:
        sel_ref = next(it)
    p_scrs = [next(it) for _ in range(PROJ_SUBTILES)]
    if rotary_gate:
        km_scr = next(it)
    tile = MOBA_BLOCK

    for sub, p_scr in enumerate(p_scrs):
        u = (_rms_scale(h_ref[0, sub * tile:(sub + 1) * tile, :]) * g_ref[...]).astype(BF16)
        p_scr[...] = _dot_nt(w_ref[...], u)

    if rotary_gate:
        @pl.when(pl.program_id(1) == 0)
        def _():
            km_scr[...] = jnp.zeros_like(km_scr)

    for sub, p_scr in enumerate(p_scrs):
        lanes = slice(sub * tile, (sub + 1) * tile)
        if rotary_gate:
            own = pl.program_id(1) * PROJ_SUBTILES + sub
            _rotary_and_gate(p_scr, own, pos_ref[0, :, lanes], freq_ref, sel_ref, km_scr, lanes)
        q_scale = QK_SCALE * LOG2_E if rotary_gate else -QK_SCALE * LOG2_E
        q_ref[0, :, lanes] = (p_scr[0:SELF_WIDTH, :] * q_scale).astype(BF16)
        for hp in range(N_SELF_HEADS // 2):
            r0 = SELF_WIDTH + hp * 2 * HEAD_DIM
            k_ref[0, hp, lanes, :] = p_scr[r0:r0 + 2 * HEAD_DIM, :].T.astype(BF16)
        v_ref[0, :, lanes] = p_scr[2 * SELF_WIDTH:3 * SELF_WIDTH, :].astype(BF16)
        _memory_attention(p_scr, kvm_ref, om_ref, lanes)


def _proj(h, g, w_in_t, kvm, rope=None):
    B, S, D = h.shape
    pw = w_in_t.shape[0]
    tm = MOBA_BLOCK * PROJ_SUBTILES
    assert S % tm == 0
    n_blk = S // MOBA_BLOCK
    grid = (B, S // tm)
    const2 = lambda b, i: (0, 0)
    in_specs = [pl.BlockSpec((1, tm, D), lambda b, i: (b, i, 0)),
                pl.BlockSpec((1, D), const2),
                pl.BlockSpec(w_in_t.shape, const2, pipeline_mode=pl.Buffered(1)),
                pl.BlockSpec((1,) + kvm.shape[1:], lambda b, i: (b, 0, 0))]
    args = [h, g.reshape(1, D), w_in_t, kvm]
    slab = lambda rows: pl.BlockSpec((1, rows, tm), lambda b, i: (b, 0, i))
    n_pairs = N_SELF_HEADS // 2
    out_shape = [jax.ShapeDtypeStruct((B, SELF_WIDTH, S), BF16),
                 jax.ShapeDtypeStruct((B, n_pairs, S, 2 * HEAD_DIM), BF16),
                 jax.ShapeDtypeStruct((B, SELF_WIDTH, S), BF16),
                 jax.ShapeDtypeStruct((B, MEM_WIDTH, S), BF16)]
    out_specs = [slab(SELF_WIDTH),
                 pl.BlockSpec((1, n_pairs, tm, 2 * HEAD_DIM), lambda b, i: (b, 0, i, 0)),
                 slab(SELF_WIDTH), slab(MEM_WIDTH)]
    scratch = [pltpu.VMEM((pw, MOBA_BLOCK), F32)] * PROJ_SUBTILES
    if rope is not None:
        positions, inv_freq = rope
        in_specs += [pl.BlockSpec((1, 1, tm), lambda b, i: (b, 0, i)),
                     pl.BlockSpec((ROT_HALF, 1), const2)]
        args += [positions.reshape(B, 1, S), inv_freq.reshape(ROT_HALF, 1)]
        out_shape.append(jax.ShapeDtypeStruct((B, N_SELF_HEADS, n_blk, S), BF16))
        out_specs.append(pl.BlockSpec((1, N_SELF_HEADS, n_blk, tm), lambda b, i: (b, 0, 0, i)))
        scratch.append(pltpu.VMEM((SELF_WIDTH, n_blk), F32))
    return pl.pallas_call(
        functools.partial(_proj_body, rotary_gate=rope is not None),
        out_shape=out_shape,
        grid=grid,
        in_specs=in_specs,
        out_specs=out_specs,
        scratch_shapes=scratch,
        compiler_params=pltpu.CompilerParams(dimension_semantics=("arbitrary", "arbitrary"),
                                             vmem_limit_bytes=VMEM_LIMIT),
        name="proj_moba" if rope is not None else "proj_sb",
    )(*args)


def _pair_queries(q):
    d, t = HEAD_DIM, q.shape[1]
    zero = jnp.zeros((d, t), q.dtype)
    return jnp.concatenate([jnp.concatenate([q[:d], zero], axis=1), jnp.concatenate([zero, q[d:]], axis=1)], axis=0)


def _moba_body(q_ref, k_ref, v_ref, sel_ref, o_ref, bias_scr, *s_scr):
    t, grp, d, pairs = ATT_TILE, MOBA_GROUP, HEAD_DIM, range(MOBA_PAIRS)
    i = pl.program_id(2)
    qq = [_pair_queries(q_ref[0, 2 * d * pr:2 * d * (pr + 1)]) for pr in pairs]
    bias_scr[...] = jnp.concatenate([sel_ref[0, hd] for hd in range(2 * MOBA_PAIRS)], axis=1).astype(F32)
    ones = jnp.ones((ONES_ROWS, t), BF16)

    def bias(j, pr):
        return bias_scr[pl.ds(j, 1), 2 * t * pr:2 * t * (pr + 1)]

    def weighted_values(pr, off, p):
        out = []
        for hd in range(2):
            r0 = (2 * pr + hd) * d
            v1 = jnp.concatenate([v_ref[0, r0:r0 + d, pl.ds(off, t)], ones], axis=0)
            out.append(jnp.dot(v1, p[:, hd * t:(hd + 1) * t], preferred_element_type=F32))
        return jnp.concatenate(out, axis=1)

    def score_block(jg, g, pr):
        off = pl.multiple_of((jg * grp + g) * t, t)
        s = jnp.dot(k_ref[0, pr, pl.ds(off, t), :], qq[pr], preferred_element_type=F32)
        s_scr[g * MOBA_PAIRS + pr][...] = s
        return jnp.max(s, axis=0, keepdims=True) + bias(jg * grp + g, pr)

    def score_group(jg):
        cm = [None] * MOBA_PAIRS
        for g in range(grp):
            for pr in pairs:
                c = score_block(jg, g, pr)
                cm[pr] = c if cm[pr] is None else jnp.maximum(cm[pr], c)
        return tuple(cm)

    off_i = pl.multiple_of(i * t, t)
    key = lax.broadcasted_iota(jnp.int32, (t, t), 0)
    qry = lax.broadcasted_iota(jnp.int32, (t, t), 1)
    causal = jnp.concatenate([key <= qry] * 2, axis=1)
    s_own = [jnp.dot(k_ref[0, pr, pl.ds(off_i, t), :], qq[pr], preferred_element_type=F32) for pr in pairs]
    cm0 = score_group(0)
    s_own = [jnp.where(causal, s_own[pr], -jnp.inf) for pr in pairs]
    m = [jnp.max(s_own[pr], axis=0, keepdims=True) for pr in pairs]
    acc = [weighted_values(pr, off_i, jnp.exp2(s_own[pr] - m[pr]).astype(BF16)) for pr in pairs]

    def weigh(n, carry, refill):
        m, acc, cm = carry
        m_new = [jnp.maximum(m[pr], cm[pr]) for pr in pairs]
        pv, cm_next = [None] * MOBA_PAIRS, list(cm)
        for g in range(grp):
            off = pl.multiple_of((n * grp + g) * t, t)
            for pr in pairs:
                s = s_scr[g * MOBA_PAIRS + pr][...]
                if refill:
                    c = score_block(n + 1, g, pr)
                    cm_next[pr] = c if g == 0 else jnp.maximum(cm_next[pr], c)
                p = jnp.exp2(s + (bias(n * grp + g, pr) - m_new[pr])).astype(BF16)
                w = weighted_values(pr, off, p)
                pv[pr] = w if pv[pr] is None else pv[pr] + w
        acc = [jnp.exp2(m[pr] - m_new[pr]) * acc[pr] + pv[pr] for pr in pairs]
        return tuple(m_new), tuple(acc), tuple(cm_next)

    last = jnp.maximum(lax.div(i + (grp - 1), grp) - 1, 0)
    carry = lax.fori_loop(0, last, lambda n, c: weigh(n, c, True), (tuple(m), tuple(acc), cm0))
    _, acc, _ = weigh(last, carry, False)
    for pr in pairs:
        out = acc[pr][:d] * (1.0 / acc[pr][d:d + 1])
        o_ref[0, 2 * d * pr:2 * d * (pr + 1)] = jnp.concatenate([out[:, :t], out[:, t:]], axis=0).astype(BF16)


def _moba_attention(q, k, v, sel):
    B, W, S = q.shape
    t = ATT_TILE
    n_blk = S // t
    assert n_blk % MOBA_GROUP == 0 and N_SELF_HEADS % (2 * MOBA_PAIRS) == 0
    rows = 2 * HEAD_DIM * MOBA_PAIRS
    q_tile = pl.BlockSpec((1, rows, t), lambda b, hq, i: (b, hq, i))
    resident = dict(pipeline_mode=pl.Buffered(1))
    return pl.pallas_call(
        _moba_body,
        out_shape=jax.ShapeDtypeStruct((B, W, S), BF16),
        grid=(B, N_SELF_HEADS // (2 * MOBA_PAIRS), n_blk),
        in_specs=[q_tile,
                  pl.BlockSpec((1, MOBA_PAIRS, S, 2 * HEAD_DIM), lambda b, hq, i: (b, hq, 0, 0), **resident),
                  pl.BlockSpec((1, rows, S), lambda b, hq, i: (b, hq, 0), **resident),
                  pl.BlockSpec((1, 2 * MOBA_PAIRS, n_blk, t), lambda b, hq, i: (b, hq, 0, i))],
        out_specs=q_tile,
        scratch_shapes=([pltpu.VMEM((n_blk, 2 * t * MOBA_PAIRS), F32)]
                        + [pltpu.VMEM((t, 2 * t), F32)] * (MOBA_GROUP * MOBA_PAIRS)),
        compiler_params=pltpu.CompilerParams(dimension_semantics=("arbitrary",) * 3, vmem_limit_bytes=VMEM_LIMIT),
        name="moba_attn",
    )(q, k, v, sel)


def _sb_body(q_ref, k_ref, v_ref, o_ref):
    t, d, pairs = ATT_TILE, HEAD_DIM, range(SB_PAIRS)
    i = pl.program_id(2)
    qq = [_pair_queries(q_ref[0, 2 * d * pr:2 * d * (pr + 1)]) for pr in pairs]
    key = lax.broadcasted_iota(jnp.int32, (t, t), 0)
    qry = lax.broadcasted_iota(jnp.int32, (t, t), 1)
    tri = (qry >= key).astype(BF16)
    tri2 = jnp.concatenate([tri, tri], axis=1)
    causal = jnp.concatenate([key < qry] * 2, axis=1)

    def scores(pr, off):
        return jnp.dot(k_ref[0, pr, pl.ds(off, t), :], qq[pr], preferred_element_type=F32)

    def suffix_sums(zn):
        log_1m = jnp.minimum(zn, 0.0) - jnp.log(1.0 + jnp.exp2(-jnp.abs(zn))) * LOG2_E
        hi = log_1m.astype(BF16)
        lo = (log_1m - hi.astype(F32)).astype(BF16)
        return jnp.dot(tri2, jnp.concatenate([hi, lo], axis=0), preferred_element_type=F32)

    def weighted_values(pr, off, a):
        out = []
        for hd in range(2):
            r0 = (2 * pr + hd) * d
            out.append(jnp.dot(v_ref[0, r0:r0 + d, pl.ds(off, t)], a[:, hd * t:(hd + 1) * t],
                               preferred_element_type=F32))
        return jnp.concatenate(out, axis=1)

    off0 = pl.multiple_of(i * t, t)
    off1 = pl.multiple_of(jnp.maximum(i - 1, 0) * t, t)
    has_prev = i >= 1
    z0 = [jnp.where(causal, scores(pr, off0), jnp.inf) for pr in pairs]
    z1 = [scores(pr, off1) for pr in pairs]
    incl0 = [suffix_sums(z0[pr]) for pr in pairs]
    incl1 = [suffix_sums(z1[pr]) for pr in pairs]
    carry, acc = [], []
    for pr in pairs:
        a0 = jnp.exp2(incl0[pr] - z0[pr])
        carry0 = incl0[pr][0:1, :]
        a1 = jnp.exp2((incl1[pr] + (carry0 + jnp.where(has_prev, 0.0, NEG_BIAS))) - z1[pr])
        acc.append(weighted_values(pr, off0, a0.astype(BF16)) + weighted_values(pr, off1, a1.astype(BF16)))
        carry.append(carry0 + jnp.where(has_prev, incl1[pr][0:1, :], 0.0))

    def liveness(carry):
        live = jnp.max(carry[0])
        for pr in pairs[1:]:
            live = jnp.maximum(live, jnp.max(carry[pr]))
        return live

    def alive(state):
        j, _, _, live = state
        return jnp.logical_and(j >= 0, live > SB_DEAD_LOG2)

    def step(state):
        j, carry, acc, _ = state
        off = pl.multiple_of(j * t, t)
        carry, acc = list(carry), list(acc)
        z = [scores(pr, off) for pr in pairs]
        for pr in pairs:
            incl = suffix_sums(z[pr])
            acc[pr] = acc[pr] + weighted_values(pr, off, jnp.exp2((incl + carry[pr]) - z[pr]).astype(BF16))
            carry[pr] = carry[pr] + incl[0:1, :]
        return j - 1, tuple(carry), tuple(acc), liveness(carry)

    _, _, acc, _ = lax.while_loop(alive, step, (i - 2, tuple(carry), tuple(acc), liveness(carry)))
    for pr in pairs:
        o_ref[0, 2 * d * pr:2 * d * (pr + 1)] = jnp.concatenate([acc[pr][:, :t], acc[pr][:, t:]], axis=0).astype(BF16)


def _sb_attention(q, k, v):
    B, W, S = q.shape
    t = ATT_TILE
    assert N_SELF_HEADS % (2 * SB_PAIRS) == 0
    rows = 2 * HEAD_DIM * SB_PAIRS
    q_tile = pl.BlockSpec((1, rows, t), lambda b, hq, i: (b, hq, i))
    resident = dict(pipeline_mode=pl.Buffered(1))
    return pl.pallas_call(
        _sb_body,
        out_shape=jax.ShapeDtypeStruct((B, W, S), BF16),
        grid=(B, N_SELF_HEADS // (2 * SB_PAIRS), S // t),
        in_specs=[q_tile,
                  pl.BlockSpec((1, SB_PAIRS, S, 2 * HEAD_DIM), lambda b, hq, i: (b, hq, 0, 0), **resident),
                  pl.BlockSpec((1, rows, S), lambda b, hq, i: (b, hq, 0), **resident)],
        out_specs=q_tile,
        compiler_params=pltpu.CompilerParams(dimension_semantics=("arbitrary",) * 3, vmem_limit_bytes=VMEM_LIMIT),
        name="sb_attn",
    )(q, k, v)


def kernel(x, mem, positions, g_ffn_pre, w_pre_gate, w_pre_up, w_pre_down, g_mix, w_in, g_mem, w_mem_kv, w_out,
           g_ffn_post, w_post_gate, w_post_up, w_post_down, g_final):
    depth = w_in.shape[0]
    inv_freq = ROPE_THETA ** (-jnp.arange(0, 2 * ROT_HALF, 2, dtype=F32) / (2 * ROT_HALF))
    h = x
    for layer in range(depth):
        h = _ffn(h, g_ffn_pre[layer], w_pre_gate[layer].astype(BF16), w_pre_up[layer].astype(BF16),
                 w_pre_down[layer].astype(BF16))
        kvm = _memkv(mem, g_mem[layer], w_mem_kv[layer].T.astype(BF16))
        w_in_t = w_in[layer].T.astype(BF16)
        if layer % 2 == 0:
            q, k, v, o_mem, sel = _proj(h, g_mix[layer], w_in_t, kvm, rope=(positions, inv_freq))
            o_self = _moba_attention(q, k, v, sel)
        else:
            q, k, v, o_mem = _proj(h, g_mix[layer], w_in_t, kvm)
            o_self = _sb_attention(q, k, v)
        w_o = w_out[layer].astype(BF16)
        h = _ffn(h, g_ffn_post[layer], w_post_gate[layer].astype(BF16), w_post_up[layer].astype(BF16),
                 w_post_down[layer].astype(BF16),
                 attn=(o_self, o_mem, w_o[:SELF_WIDTH], w_o[SELF_WIDTH:]),
                 final_g=g_final if layer == depth - 1 else None)
    return h
```
